```python
import jax, jax.numpy as jnp
from jax import lax
import numpy as np

D_MODEL = 1024
BATCH = 16
SEQ = 2048
DEPTH = 4
DEC_BATCH = 2
DEC_SEQ = 16384
PAST_LEN = 128

HEAD_DIM = 64
N_GROUPS = 4
MIX_W = D_MODEL
GROUP_W = MIX_W // N_GROUPS
GROUP_HEADS = GROUP_W // HEAD_DIM
A_HEADS = GROUP_HEADS
A_CONFIGS = ((128, 1), (512, 4), (2048, 16))
A_BLOCK = 64
B_HEADS = GROUP_HEADS
B_KV_HEADS = 2
B_HALF_WINDOW = 128
B_BLOCK = 128
C_HEADS = GROUP_HEADS
C_Q_RANK = 256
C_KV_RANK = 128
C_NOPE = 64
C_ROPE = 32
C_V = HEAD_DIM
C_QBLOCK = 128
ROPE_THETA = 10000.0
D_HEADS = GROUP_HEADS
GRID_W = 64
NA_KH = 8
NA_KW = 16
D_FF = 4 * D_MODEL
EPS = 1e-5
NEG_INF = -1e30
N_ALIBI_HEADS = A_HEADS + B_HEADS
IN_SECTIONS = (GROUP_W, GROUP_W, GROUP_W,
               GROUP_W, B_KV_HEADS * HEAD_DIM, B_KV_HEADS * HEAD_DIM,
               C_Q_RANK, C_KV_RANK, C_ROPE,
               GROUP_W, GROUP_W, GROUP_W)
IN_W = sum(IN_SECTIONS)

kernel_name = 'hybrid_parallel_group_encoder'


def rmsnorm(x, g):
    xf = x.astype(jnp.float32)
    y = xf * lax.rsqrt(jnp.mean(xf * xf, axis=-1, keepdims=True) + EPS)
    return (y * g.astype(jnp.float32)).astype(x.dtype)


def split_points():
    return np.cumsum(np.array(IN_SECTIONS))[:-1].tolist()


def alibi_slopes():
    return jnp.exp2(-8.0 * jnp.arange(1, N_ALIBI_HEADS + 1, dtype=jnp.float32) / N_ALIBI_HEADS)


def rope_tables(S):
    inv_freq = 1.0 / (ROPE_THETA ** (jnp.arange(0, C_ROPE, 2, dtype=jnp.float32) / C_ROPE))
    ang = jnp.arange(S, dtype=jnp.float32)[:, None] * inv_freq[None, :]
    return jnp.cos(ang), jnp.sin(ang)


def rotate(x, cos, sin):
    x1, x2 = jnp.split(x.astype(jnp.float32), 2, axis=-1)
    return jnp.concatenate([x1 * cos - x2 * sin, x1 * sin + x2 * cos], axis=-1).astype(x.dtype)


def banded_attention(q, k, v, half_window, block, slopes, dist_scale, sink=None):
    N, L, H, dh = q.shape
    Hkv = k.shape[2]
    G = H // Hkv
    nb = -(-L // block)
    Lp = nb * block
    pad = Lp - L
    qb = jnp.pad(q, ((0, 0), (0, pad), (0, 0), (0, 0))).reshape(N, nb, block, Hkv, G, dh)

    def windows(a):
        ap = jnp.pad(a, ((0, 0), (block, pad + block), (0, 0), (0, 0))).reshape(N, nb + 2, block, Hkv, dh)
        return jnp.concatenate([ap[:, :-2], ap[:, 1:-1], ap[:, 2:]], axis=2)

    kw, vw = windows(k), windows(v)
    qi = jnp.arange(block)
    kj = jnp.arange(3 * block)
    rel = kj[None, :] - block - qi[:, None]
    kpos = (jnp.arange(nb)[:, None] - 1) * block + kj[None, :]
    mask = (jnp.abs(rel) <= half_window)[None] & ((kpos >= 0) & (kpos < L))[:, None, :]
    s = jnp.einsum('nbqkgd,nbskd->nbkgqs', qb, kw, preferred_element_type=jnp.float32) * (dh ** -0.5)
    dist = jnp.abs(rel).astype(jnp.float32) * dist_scale
    s = s - slopes.astype(jnp.float32).reshape(Hkv, G)[:, :, None, None] * dist[None, None]
    s = jnp.where(mask[None, :, None, None], s, NEG_INF)
    m = jnp.max(s, axis=-1, keepdims=True)
    if sink is not None:
        sink_b = sink.astype(jnp.float32).reshape(Hkv, G)[:, :, None, None]
        m = jnp.maximum(m, sink_b)
    p = jnp.exp(s - m)
    den = jnp.sum(p, axis=-1, keepdims=True)
    if sink is not None:
        den = den + jnp.exp(sink_b - m)
    p = p / den
    o = jnp.einsum('nbkgqs,nbskd->nbqkgd', p.astype(v.dtype), vw, preferred_element_type=jnp.float32)
    o = o.reshape(N, Lp, H, dh)[:, :L].astype(q.dtype)
    lse = (m + jnp.log(den))[..., 0].transpose(0, 1, 4, 2, 3).reshape(N, Lp, H)[:, :L]
    return o, lse


def dilated_attention(q, k, v, slopes):
    B, S, H, dh = q.shape
    outs, lses = [], []
    for window, dil in A_CONFIGS:
        L = S // dil

        def to_strided(a):
            return a.reshape(B, L, dil, H, dh).transpose(0, 2, 1, 3, 4).reshape(B * dil, L, H, dh)

        o, lse = banded_attention(to_strided(q), to_strided(k), to_strided(v),
                                  window // (2 * dil), A_BLOCK, slopes, float(dil))
        outs.append(o.reshape(B, dil, L, H, dh).transpose(0, 2, 1, 3, 4).reshape(B, S, H, dh))
        lses.append(lse.reshape(B, dil, L, H).transpose(0, 2, 1, 3).reshape(B, S, H))
    w = jax.nn.softmax(jnp.stack(lses, axis=0), axis=0)
    o = jnp.sum(w[..., None] * jnp.stack(outs, axis=0).astype(jnp.float32), axis=0)
    return o.astype(q.dtype)


def latent_attention(c_q, c_kv, k_rope, q_norm, w_q_up, kv_norm, w_kv_up, cos, sin):
    B, S, _ = c_q.shape
    q = jnp.einsum('bsr,re->bse', rmsnorm(c_q, q_norm), w_q_up).reshape(B, S, C_HEADS, C_NOPE + C_ROPE)
    q_nope = q[..., :C_NOPE]
    q_pe = rotate(q[..., C_NOPE:], cos[:, None], sin[:, None])
    kv = jnp.einsum('bsr,re->bse', rmsnorm(c_kv, kv_norm), w_kv_up).reshape(B, S, C_HEADS, C_NOPE + C_V)
    k_nope, v = kv[..., :C_NOPE], kv[..., C_NOPE:]
    k_pe = rotate(k_rope, cos, sin)
    scale = (C_NOPE + C_ROPE) ** -0.5
    nq = S // C_QBLOCK
    qn_blocks = q_nope.reshape(B, nq, C_QBLOCK, C_HEADS, C_NOPE).transpose(1, 0, 2, 3, 4)
    qp_blocks = q_pe.reshape(B, nq, C_QBLOCK, C_HEADS, C_ROPE).transpose(1, 0, 2, 3, 4)

    def attend(blk):
        qn_b, qp_b = blk
        s = (jnp.einsum('bqhd,bshd->bhqs', qn_b, k_nope, preferred_element_type=jnp.float32)
             + jnp.einsum('bqhr,bsr->bhqs', qp_b, k_pe, preferred_element_type=jnp.float32)) * scale
        p = jax.nn.softmax(s, axis=-1)
        return jnp.einsum('bhqs,bshd->bqhd', p.astype(v.dtype), v,
                          preferred_element_type=jnp.float32).astype(v.dtype)

    o = lax.map(attend, (qn_blocks, qp_blocks))
    return o.transpose(1, 0, 2, 3, 4).reshape(B, S, C_HEADS, C_V)


def neighborhood_attention(q, k, v, rpb):
    B, S, H, dh = q.shape
    rows = S // GRID_W
    kh = min(NA_KH, rows)
    r = jnp.arange(rows)
    row_start = jnp.clip(r - kh // 2, 0, rows - kh)
    row_idx = row_start[:, None] + jnp.arange(kh)[None, :]
    c = jnp.arange(GRID_W)
    col_start = jnp.clip(c - NA_KW // 2, 0, GRID_W - NA_KW)
    col_mask = (c[None, :] >= col_start[:, None]) & (c[None, :] < col_start[:, None] + NA_KW)
    qg = q.reshape(B, rows, GRID_W, H, dh)
    kg = k.reshape(B, rows, GRID_W, H, dh)[:, row_idx]
    vg = v.reshape(B, rows, GRID_W, H, dh)[:, row_idx]
    dr = row_idx - r[:, None] + (NA_KH - 1)
    dc = jnp.clip(c[None, :] - c[:, None], -(NA_KW - 1), NA_KW - 1) + (NA_KW - 1)
    bias = rpb.astype(jnp.float32)[:, dr][:, :, :, dc]
    bias = bias.transpose(0, 1, 3, 2, 4)
    bias = jnp.where(col_mask[None, None, :, None, :], bias, NEG_INF)
    s = jnp.einsum('brchd,briwhd->bhrciw', qg, kg, preferred_element_type=jnp.float32) * (dh ** -0.5)
    s = s + bias[None]
    p = jax.nn.softmax(s.reshape(B, H, rows, GRID_W, kh * GRID_W), axis=-1)
    p = p.reshape(B, H, rows, GRID_W, kh, GRID_W)
    o = jnp.einsum('bhrciw,briwhd->brchd', p.astype(v.dtype), vg, preferred_element_type=jnp.float32)
    return o.reshape(B, S, H, dh).astype(q.dtype)


def encoder_layer(x, lp, cos, sin, slopes_a, slopes_b):
    (norm_attn, w_in, mla_q_norm, w_q_up, mla_kv_norm, w_kv_up, sink_logits, na_rpb,
     group_norm, w_out, norm_mlp, w_mlp_up, w_mlp_down) = lp
    B, S, _ = x.shape
    h = rmsnorm(x, norm_attn)
    proj = jnp.einsum('bsd,de->bse', h, w_in)
    a_q, a_k, a_v, b_q, b_k, b_v, c_q, c_kv, c_kr, d_q, d_k, d_v = jnp.split(proj, split_points(), axis=-1)

    def heads(t):
        return t.reshape(B, S, -1, HEAD_DIM)

    o_a = dilated_attention(heads(a_q), heads(a_k), heads(a_v), slopes_a)
    o_b, _ = banded_attention(heads(b_q), heads(b_k), heads(b_v), B_HALF_WINDOW, B_BLOCK,
                              slopes_b, 1.0, sink_logits)
    o_c = latent_attention(c_q, c_kv, c_kr, mla_q_norm, w_q_up, mla_kv_norm, w_kv_up, cos, sin)
    o_d = neighborhood_attention(heads(d_q), heads(d_k), heads(d_v), na_rpb)
    o = jnp.concatenate([o_a.reshape(B, S, GROUP_W), o_b.reshape(B, S, GROUP_W),
                         o_c.reshape(B, S, GROUP_W), o_d.reshape(B, S, GROUP_W)], axis=-1)
    o = rmsnorm(o.reshape(B, S, N_GROUPS, GROUP_W), group_norm).reshape(B, S, MIX_W)
    x = x + jnp.einsum('bse,ed->bsd', o, w_out)
    h = rmsnorm(x, norm_mlp)
    u = jnp.square(jax.nn.relu(jnp.einsum('bsd,df->bsf', h, w_mlp_up)))
    return x + jnp.einsum('bsf,fd->bsd', u, w_mlp_down)


def encoder_trunk(x, params, norm_final):
    S = x.shape[1]
    cos, sin = rope_tables(S)
    slopes = alibi_slopes()
    slopes_a, slopes_b = slopes[1::2], slopes[0::2]
    for layer in range(DEPTH):
        x = encoder_layer(x, tuple(p[layer] for p in params), cos, sin, slopes_a, slopes_b)
    return rmsnorm(x, norm_final)


def setup_inputs(seed: int = 0) -> dict:
    key = jax.random.key(seed)
    ks = jax.random.split(key, 16)

    def normal(k, shape, scale):
        return scale * jax.random.normal(k, shape, jnp.float32)

    def gain(k, shape):
        return 1.0 + 0.05 * jax.random.normal(k, shape, jnp.float32)

    return {
        'x_prompt': normal(ks[0], (BATCH, SEQ, D_MODEL), 1.0),
        'x_sample': normal(ks[1], (DEC_BATCH, DEC_SEQ, D_MODEL), 1.0),
        'norm_attn': gain(ks[2], (DEPTH, D_MODEL)),
        'w_in': normal(ks[3], (DEPTH, D_MODEL, IN_W), D_MODEL ** -0.5),
        'mla_q_norm': gain(ks[4], (DEPTH, C_Q_RANK)),
        'w_q_up': normal(ks[5], (DEPTH, C_Q_RANK, C_HEADS * (C_NOPE + C_ROPE)), C_Q_RANK ** -0.5),
        'mla_kv_norm': gain(ks[6], (DEPTH, C_KV_RANK)),
        'w_kv_up': normal(ks[7], (DEPTH, C_KV_RANK, C_HEADS * (C_NOPE + C_V)), C_KV_RANK ** -0.5),
        'sink_logits': normal(ks[8], (DEPTH, B_HEADS), 0.5),
        'na_rpb': normal(ks[9], (DEPTH, D_HEADS, 2 * NA_KH - 1, 2 * NA_KW - 1), 0.1),
        'group_norm': gain(ks[10], (DEPTH, N_GROUPS, GROUP_W)),
        'w_out': normal(ks[11], (DEPTH, MIX_W, D_MODEL), MIX_W ** -0.5),
        'norm_mlp': gain(ks[12], (DEPTH, D_MODEL)),
        'w_mlp_up': normal(ks[13], (DEPTH, D_MODEL, D_FF), D_MODEL ** -0.5),
        'w_mlp_down': normal(ks[14], (DEPTH, D_FF, D_MODEL), D_FF ** -0.5),
        'norm_final': gain(ks[15], (D_MODEL,)),
    }


def reference(x_prompt, x_sample, norm_attn, w_in, mla_q_norm, w_q_up, mla_kv_norm, w_kv_up,
              sink_logits, na_rpb, group_norm, w_out, norm_mlp, w_mlp_up, w_mlp_down, norm_final):
    params = (norm_attn, w_in, mla_q_norm, w_q_up, mla_kv_norm, w_kv_up, sink_logits, na_rpb,
              group_norm, w_out, norm_mlp, w_mlp_up, w_mlp_down)
    y_prompt = encoder_trunk(x_prompt, params, norm_final)
    y_sample = encoder_trunk(x_sample, params, norm_final)
    return (y_prompt, y_sample)
```

```python
import functools

import numpy as np
import jax
import jax.numpy as jnp
from jax import lax
from jax.experimental import pallas as pl
from jax.experimental.pallas import tpu as pltpu

F32 = jnp.float32
BF16 = jnp.bfloat16

D_MODEL = 1024
HEAD_DIM = 64
N_GROUPS = 4
GROUP_W = 256
A_CONFIGS = ((128, 1), (512, 4), (2048, 16))
B_HALF_WINDOW = 128
C_Q_RANK = 256
C_KV_RANK = 128
C_NOPE = 64
C_ROPE = 32
C_HEADS = 4
ROPE_THETA = 10000.0
GRID_W = 64
NA_KH = 8
NA_KW = 16
D_FF = 4096
EPS = 1e-5
NEG_INF = -1e30

W_A = 768
W_B = 512
W_D = 768
W_C = 640
MLA_HEAD_PAD = 128

VMEM_LIMIT = 56 * 1024 * 1024


def _cparams(n_axes):
    return pltpu.CompilerParams(dimension_semantics=("arbitrary",) * n_axes,
                                vmem_limit_bytes=VMEM_LIMIT)


def _rms(x, g):
    return x * lax.rsqrt(jnp.mean(x * x, axis=-1, keepdims=True) + EPS) * g


def _inproj_kernel(x_ref, g_ref, w_ref, oa_ref, ob_ref, od_ref, oc_ref):
    h = _rms(x_ref[...], g_ref[...]).astype(BF16)
    off = 0
    for o_ref in (oa_ref, ob_ref, od_ref, oc_ref):
        n = o_ref.shape[-1]
        o_ref[...] = jnp.dot(h, w_ref[:, off:off + n], preferred_element_type=F32).astype(o_ref.dtype)
        off += n


def _inproj(x2, g, w, tm=512):
    t = x2.shape[0]
    widths = (W_A, W_B, W_D, W_C)
    return pl.pallas_call(
        _inproj_kernel,
        grid=(t // tm,),
        in_specs=[pl.BlockSpec((tm, D_MODEL), lambda i: (i, 0)),
                  pl.BlockSpec((1, D_MODEL), lambda i: (0, 0)),
                  pl.BlockSpec((D_MODEL, sum(widths)), lambda i: (0, 0))],
        out_specs=[pl.BlockSpec((tm, n), lambda i: (i, 0)) for n in widths],
        out_shape=[jax.ShapeDtypeStruct((t, n), BF16) for n in widths],
        compiler_params=_cparams(1),
        name="inproj",
    )(x2, g, w)


def _win_attn_kernel(*refs, heads, ck, has_sink, has_lse):
    q_ref, kp_ref, km_ref, kn_ref, vp_ref, vm_ref, vn_ref, bias_ref = refs[:8]
    pos = 8
    sink_ref = None
    if has_sink:
        sink_ref = refs[pos]
        pos += 1
    o_ref = refs[pos]
    lse_ref = refs[pos + 1] if has_lse else None

    q = q_ref[0]
    tq = q.shape[0]
    k = jnp.concatenate([kp_ref[0], km_ref[0], kn_ref[0]], axis=0)
    v = jnp.concatenate([vp_ref[0], vm_ref[0], vn_ref[0]], axis=0)
    lane = lax.broadcasted_iota(jnp.int32, (tq, ck), 1)
    masks = [(lane >= lo) & (lane < lo + HEAD_DIM) for (_, lo) in heads]
    q_stack = jnp.concatenate(
        [jnp.where(mk, q[:, g * ck:(g + 1) * ck], jnp.zeros((), q.dtype))
         for (g, _), mk in zip(heads, masks)], axis=0)
    s = lax.dot_general(q_stack, k, (((1,), (1,)), ((), ())), preferred_element_type=F32)
    s = s * (HEAD_DIM ** -0.5) + bias_ref[0]
    m = jnp.max(s, axis=-1, keepdims=True)
    if has_sink:
        sink = jnp.concatenate([jnp.full((tq, 1), sink_ref[h], F32) for h in range(len(heads))], axis=0)
        m = jnp.maximum(m, sink)
    p = jnp.exp(s - m)
    den = jnp.sum(p, axis=-1, keepdims=True)
    if has_sink:
        den = den + jnp.exp(sink - m)
    pv = jnp.dot(p.astype(v.dtype), v, preferred_element_type=F32) / den
    n_groups = 1 + max(g for g, _ in heads)
    outs = [jnp.zeros((tq, ck), F32) for _ in range(n_groups)]
    for h, ((g, _), mk) in enumerate(zip(heads, masks)):
        outs[g] = jnp.where(mk, pv[h * tq:(h + 1) * tq], outs[g])
    o_ref[0] = jnp.concatenate(outs, axis=1).astype(o_ref.dtype)
    if has_lse:
        lse = m + jnp.log(den)
        louts = [jnp.zeros((tq, ck), F32) for _ in range(n_groups)]
        for h, ((g, _), mk) in enumerate(zip(heads, masks)):
            louts[g] = jnp.where(mk, lse[h * tq:(h + 1) * tq], louts[g])
        lse_ref[0] = jnp.concatenate(louts, axis=1)


def _win_attn(arr, bias, sink, *, n_inner, seq, tq, hb, cq, ck, qcol, kcol, vcol,
              out_cols, heads, has_lse, name):
    b = arr.shape[0]
    nt = seq // tq
    per = tq // hb
    nhb = seq // hb
    nh = len(heads)
    tkw = tq + 2 * hb

    def tile_type(i):
        return jnp.where(i == 0, 0, jnp.where(i == nt - 1, 2, 1))

    def prev_idx(i):
        return jnp.maximum(i * per - 1, 0)

    def next_idx(i):
        return jnp.minimum((i + 1) * per, nhb - 1)

    in_specs = [
        pl.BlockSpec((1, tq, cq), lambda n, r, i: (n, i, qcol(r))),
        pl.BlockSpec((1, hb, ck), lambda n, r, i: (n, prev_idx(i), kcol(r))),
        pl.BlockSpec((1, tq, ck), lambda n, r, i: (n, i, kcol(r))),
        pl.BlockSpec((1, hb, ck), lambda n, r, i: (n, next_idx(i), kcol(r))),
        pl.BlockSpec((1, hb, ck), lambda n, r, i: (n, prev_idx(i), vcol(r))),
        pl.BlockSpec((1, tq, ck), lambda n, r, i: (n, i, vcol(r))),
        pl.BlockSpec((1, hb, ck), lambda n, r, i: (n, next_idx(i), vcol(r))),
        pl.BlockSpec((1, nh * tq, tkw), lambda n, r, i: (tile_type(i), 0, 0)),
    ]
    args = [arr] * 7 + [bias]
    if sink is not None:
        in_specs.append(pl.BlockSpec(memory_space=pltpu.SMEM))
        args.append(sink)
    out_specs = [pl.BlockSpec((1, tq, cq), lambda n, r, i: (n, i, r))]
    out_shape = [jax.ShapeDtypeStruct((b, seq, out_cols), BF16)]
    if has_lse:
        out_specs.append(pl.BlockSpec((1, tq, cq), lambda n, r, i: (n, i, r)))
        out_shape.append(jax.ShapeDtypeStruct((b, seq, out_cols), F32))
    return pl.pallas_call(
        functools.partial(_win_attn_kernel, heads=heads, ck=ck,
                          has_sink=sink is not None, has_lse=has_lse),
        grid=(b, n_inner, nt),
        in_specs=in_specs,
        out_specs=out_specs,
        out_shape=out_shape,
        compiler_params=_cparams(3),
        name=name,
    )(*args)


def _tile_indices(nt):
    return (0, min(1, nt - 1), nt - 1)


def _banded_bias(seq, tq, hb, hw, slopes, dist_scale):
    nt = seq // tq
    tabs = []
    for ti in _tile_indices(nt):
        qpos = ti * tq + np.arange(tq)[:, None]
        kpos = ti * tq - hb + np.arange(tq + 2 * hb)[None, :]
        dist = np.abs(kpos - qpos)
        valid = (dist <= hw) & (kpos >= 0) & (kpos < seq)
        pen = -np.asarray(slopes, np.float32)[:, None, None] * (dist.astype(np.float32) * np.float32(dist_scale))[None]
        tabs.append(np.where(valid[None], pen, np.float32(NEG_INF)).reshape(-1, tq + 2 * hb))
    return jnp.asarray(np.stack(tabs).astype(np.float32))


def _na_bias(rpb, seq, rows_per_tile):
    rows = seq // GRID_W
    kh = min(NA_KH, rows)
    nt = rows // rows_per_tile
    tq = rows_per_tile * GRID_W
    tabs = []
    for ti in _tile_indices(nt):
        r0 = ti * rows_per_tile
        qi = np.arange(tq)
        kj = np.arange(3 * tq)
        rq, cq = (r0 + qi // GRID_W)[:, None], (qi % GRID_W)[:, None]
        rk, ck = (r0 - rows_per_tile + kj // GRID_W)[None, :], (kj % GRID_W)[None, :]
        row_start = np.clip(rq - kh // 2, 0, rows - kh)
        col_start = np.clip(cq - NA_KW // 2, 0, GRID_W - NA_KW)
        valid = ((rk >= row_start) & (rk < row_start + kh) & (ck >= col_start) & (ck < col_start + NA_KW))
        dr = np.clip(rk - rq + (NA_KH - 1), 0, 2 * NA_KH - 2)
        dc = np.clip(ck - cq, -(NA_KW - 1), NA_KW - 1) + (NA_KW - 1)
        vals = rpb.astype(F32)[:, dr, dc]
        tabs.append(jnp.where(valid[None], vals, NEG_INF).reshape(-1, 3 * tq))
    return jnp.stack(tabs)


def _alibi_slopes():
    s = np.exp2(-8.0 * np.arange(1, 9, dtype=np.float32) / 8.0)
    return s[1::2], s[0::2]


HEADS_FULL = ((0, 0), (0, 64), (0, 128), (0, 192))
HEADS_GQA = ((0, 0), (0, 64), (1, 0), (1, 64))
GQA_HEAD_ORDER = (0, 2, 1, 3)


def _mixer_a(qkv_a, bsz, seq):
    slopes_a, _ = _alibi_slopes()
    outs = []
    for window, dil in A_CONFIGS:
        length = seq // dil
        hw = window // (2 * dil)
        tq = min(128, length)
        bias = _banded_bias(length, tq, hw, hw, slopes_a, float(dil))
        o, lse = _win_attn(
            qkv_a.reshape(bsz, length, dil * W_A), bias, None,
            n_inner=dil, seq=length, tq=tq, hb=hw, cq=GROUP_W, ck=GROUP_W,
            qcol=lambda r: 3 * r, kcol=lambda r: 3 * r + 1, vcol=lambda r: 3 * r + 2,
            out_cols=dil * GROUP_W, heads=HEADS_FULL, has_lse=True, name=f"mix_a_d{dil}")
        outs.append((o.reshape(bsz * seq, GROUP_W), lse.reshape(bsz * seq, GROUP_W)))
    return outs


def _mixer_b(qkv_b, sink, bsz, seq):
    _, slopes_b = _alibi_slopes()
    tq = 256
    bias = _banded_bias(seq, tq, B_HALF_WINDOW, B_HALF_WINDOW, slopes_b[list(GQA_HEAD_ORDER)], 1.0)
    (o,) = _win_attn(
        qkv_b.reshape(bsz, seq, W_B), bias, sink,
        n_inner=1, seq=seq, tq=tq, hb=B_HALF_WINDOW, cq=GROUP_W, ck=128,
        qcol=lambda r: 0, kcol=lambda r: 2, vcol=lambda r: 3,
        out_cols=GROUP_W, heads=HEADS_GQA, has_lse=False, name="mix_b")
    return o.reshape(bsz * seq, GROUP_W)


def _mixer_d(qkv_d, rpb, bsz, seq):
    rows_per_tile = 4
    tq = rows_per_tile * GRID_W
    bias = _na_bias(rpb, seq, rows_per_tile)
    (o,) = _win_attn(
        qkv_d.reshape(bsz, seq, W_D), bias, None,
        n_inner=1, seq=seq, tq=tq, hb=tq, cq=GROUP_W, ck=GROUP_W,
        qcol=lambda r: 0, kcol=lambda r: 1, vcol=lambda r: 2,
        out_cols=GROUP_W, heads=HEADS_FULL, has_lse=False, name="mix_d")
    return o.reshape(bsz * seq, GROUP_W)


def _mla_prep_kernel(c_ref, ct_ref, st_ref, qn_ref, kvn_ref, wq_ref, wk_ref, wvt_ref,
                     q_ref, k_ref, vt_ref):
    hp = MLA_HEAD_PAD
    c = c_ref[0]
    ct = ct_ref[...]
    st = st_ref[...]
    cqn = _rms(c[:, :C_Q_RANK].astype(F32), qn_ref[...]).astype(BF16)
    qq = jnp.dot(cqn, wq_ref[...], preferred_element_type=F32)
    scale = (C_NOPE + C_ROPE) ** -0.5
    q_ref[0] = jnp.concatenate(
        [(qq[:, h * hp:(h + 1) * hp] * ct + qq[:, (C_HEADS + h) * hp:(C_HEADS + h + 1) * hp] * st) * scale
         for h in range(C_HEADS)], axis=1).astype(q_ref.dtype)
    ckvn = _rms(c[:, C_Q_RANK:C_Q_RANK + C_KV_RANK].astype(F32), kvn_ref[...]).astype(BF16)
    kk = jnp.dot(ckvn, wk_ref[...], preferred_element_type=F32)
    o0 = C_Q_RANK + C_KV_RANK
    kpe = c[:, o0:o0 + hp].astype(F32) * ct + c[:, o0 + hp:o0 + 2 * hp].astype(F32) * st
    k_ref[0] = jnp.concatenate([kk[:, h * hp:(h + 1) * hp] + kpe for h in range(C_HEADS)],
                               axis=1).astype(k_ref.dtype)
    vt_ref[0] = lax.dot_general(wvt_ref[...], ckvn, (((1,), (1,)), ((), ())),
                                preferred_element_type=F32).astype(vt_ref.dtype)


def _mla_prep(c3, ct, st, qn, kvn, wq, wk, wvt, tm=512):
    b, s, _ = c3.shape
    hw = C_HEADS * MLA_HEAD_PAD
    const = lambda shape: pl.BlockSpec(shape, lambda n, i: (0,) * len(shape))
    return pl.pallas_call(
        _mla_prep_kernel,
        grid=(b, s // tm),
        in_specs=[pl.BlockSpec((1, tm, W_C), lambda n, i: (n, i, 0)),
                  pl.BlockSpec((tm, MLA_HEAD_PAD), lambda n, i: (i, 0)),
                  pl.BlockSpec((tm, MLA_HEAD_PAD), lambda n, i: (i, 0)),
                  const((1, C_Q_RANK)), const((1, C_KV_RANK)),
                  const((C_Q_RANK, 2 * hw)), const((C_KV_RANK, hw)), const((GROUP_W, C_KV_RANK))],
        out_specs=[pl.BlockSpec((1, tm, hw), lambda n, i: (n, i, 0)),
                   pl.BlockSpec((1, tm, hw), lambda n, i: (n, i, 0)),
                   pl.BlockSpec((1, GROUP_W, tm), lambda n, i: (n, 0, i))],
        out_shape=[jax.ShapeDtypeStruct((b, s, hw), BF16),
                   jax.ShapeDtypeStruct((b, s, hw), BF16),
                   jax.ShapeDtypeStruct((b, GROUP_W, s), BF16)],
        compiler_params=_cparams(2),
        name="mla_prep",
    )(c3, ct, st, qn, kvn, wq, wk, wvt)


def _mla_flash_kernel(q_ref, k_ref, vt_ref, o_ref, m_sc, l_sc, acc_sc):
    hp = MLA_HEAD_PAD
    kv = pl.program_id(2)

    @pl.when(kv == 0)
    def _():
        m_sc[...] = jnp.full(m_sc.shape, NEG_INF, F32)
        l_sc[...] = jnp.zeros(l_sc.shape, F32)
        acc_sc[...] = jnp.zeros(acc_sc.shape, F32)

    for h in range(C_HEADS):
        st = lax.dot_general(k_ref[0, :, h * hp:(h + 1) * hp], q_ref[0, :, h * hp:(h + 1) * hp],
                             (((1,), (1,)), ((), ())), preferred_element_type=F32)
        m_old = m_sc[h:h + 1, :]
        m_new = jnp.maximum(m_old, jnp.max(st, axis=0, keepdims=True))
        alpha = jnp.exp(m_old - m_new)
        pt = jnp.exp(st - m_new)
        l_sc[h:h + 1, :] = alpha * l_sc[h:h + 1, :] + jnp.sum(pt, axis=0, keepdims=True)
        m_sc[h:h + 1, :] = m_new
        rows = slice(h * HEAD_DIM, (h + 1) * HEAD_DIM)
        acc_sc[rows, :] = alpha * acc_sc[rows, :] + jnp.dot(
            vt_ref[0, rows, :], pt.astype(BF16), preferred_element_type=F32)

    @pl.when(kv == pl.num_programs(2) - 1)
    def _():
        for h in range(C_HEADS):
            rows = slice(h * HEAD_DIM, (h + 1) * HEAD_DIM)
            acc_sc[rows, :] = acc_sc[rows, :] / l_sc[h:h + 1, :]
        o_ref[0] = acc_sc[...].T.astype(o_ref.dtype)


def _mla_flash(q, k, vt, tq=512, tk=1024):
    b, s, hw = q.shape
    tq = min(tq, s)
    tk = min(tk, s)
    return pl.pallas_call(
        _mla_flash_kernel,
        grid=(b, s // tq, s // tk),
        in_specs=[pl.BlockSpec((1, tq, hw), lambda n, i, j: (n, i, 0)),
                  pl.BlockSpec((1, tk, hw), lambda n, i, j: (n, j, 0)),
                  pl.BlockSpec((1, GROUP_W, tk), lambda n, i, j: (n, 0, j))],
        out_specs=pl.BlockSpec((1, tq, GROUP_W), lambda n, i, j: (n, i, 0)),
        out_shape=jax.ShapeDtypeStruct((b, s, GROUP_W), BF16),
        scratch_shapes=[pltpu.VMEM((8, tq), F32), pltpu.VMEM((8, tq), F32),
                        pltpu.VMEM((GROUP_W, tq), F32)],
        compiler_params=_cparams(3),
        name="mla_flash",
    )(q, k, vt)


def _rope_tables(seq):
    inv_freq = 1.0 / (ROPE_THETA ** (jnp.arange(0, C_ROPE, 2, dtype=F32) / C_ROPE))
    ang = jnp.arange(seq, dtype=F32)[:, None] * inv_freq[None, :]
    cos, sin = jnp.cos(ang), jnp.sin(ang)
    pad = MLA_HEAD_PAD - C_NOPE - C_ROPE
    ct = jnp.concatenate([jnp.ones((seq, C_NOPE), F32), cos, cos, jnp.zeros((seq, pad), F32)], axis=1)
    st = jnp.concatenate([jnp.zeros((seq, C_NOPE), F32), -sin, sin, jnp.zeros((seq, pad), F32)], axis=1)
    return ct, st


def _outproj_kernel(a1_ref, a2_ref, a3_ref, l1_ref, l2_ref, l3_ref, ob_ref, oc_ref, od_ref,
                    x_ref, g_ref, w_ref, o_ref):
    ls = [l1_ref[...], l2_ref[...], l3_ref[...]]
    m = jnp.maximum(jnp.maximum(ls[0], ls[1]), ls[2])
    es = [jnp.exp(l - m) for l in ls]
    den = es[0] + es[1] + es[2]
    oa = sum((e / den) * a_ref[...].astype(F32) for e, a_ref in zip(es, (a1_ref, a2_ref, a3_ref)))
    groups = [oa, ob_ref[...].astype(F32), oc_ref[...].astype(F32), od_ref[...].astype(F32)]
    y = jnp.concatenate([_rms(o, g_ref[i:i + 1, :]).astype(BF16) for i, o in enumerate(groups)], axis=1)
    o_ref[...] = x_ref[...] + jnp.dot(y, w_ref[...], preferred_element_type=F32)


def _outproj(a_outs, ob, oc, od, x2, g, w, tm=512):
    t = x2.shape[0]
    tok = lambda n: pl.BlockSpec((tm, n), lambda i: (i, 0))
    args = [o for o, _ in a_outs] + [l for _, l in a_outs] + [ob, oc, od, x2, g, w]
    return pl.pallas_call(
        _outproj_kernel,
        grid=(t // tm,),
        in_specs=[tok(GROUP_W)] * 9 + [tok(D_MODEL),
                                       pl.BlockSpec((N_GROUPS, GROUP_W), lambda i: (0, 0)),
                                       pl.BlockSpec((D_MODEL, D_MODEL), lambda i: (0, 0))],
        out_specs=tok(D_MODEL),
        out_shape=jax.ShapeDtypeStruct((t, D_MODEL), F32),
        compiler_params=_cparams(1),
        name="outproj",
    )(*args)


def _mlp_kernel(x_ref, g_ref, wu_ref, wd_ref, gf_ref, o_ref, *, final, ff_chunk):
    x = x_ref[...]
    h = _rms(x, g_ref[...]).astype(BF16)
    acc = x
    for c in range(D_FF // ff_chunk):
        cols = slice(c * ff_chunk, (c + 1) * ff_chunk)
        u = jnp.dot(h, wu_ref[:, cols], preferred_element_type=F32)
        u = jnp.square(jnp.maximum(u, 0.0)).astype(BF16)
        acc = acc + jnp.dot(u, wd_ref[cols, :], preferred_element_type=F32)
    if final:
        acc = _rms(acc, gf_ref[...])
    o_ref[...] = acc


def _mlp(x2, g, wu, wd, gf, final, tm=512, ff_chunk=1024):
    t = x2.shape[0]
    return pl.pallas_call(
        functools.partial(_mlp_kernel, final=final, ff_chunk=ff_chunk),
        grid=(t // tm,),
        in_specs=[pl.BlockSpec((tm, D_MODEL), lambda i: (i, 0)),
                  pl.BlockSpec((1, D_MODEL), lambda i: (0, 0)),
                  pl.BlockSpec((D_MODEL, D_FF), lambda i: (0, 0)),
                  pl.BlockSpec((D_FF, D_MODEL), lambda i: (0, 0)),
                  pl.BlockSpec((1, D_MODEL), lambda i: (0, 0))],
        out_specs=pl.BlockSpec((tm, D_MODEL), lambda i: (i, 0)),
        out_shape=jax.ShapeDtypeStruct((t, D_MODEL), F32),
        compiler_params=_cparams(1),
        name="mlp_final" if final else "mlp",
    )(x2, g, wu, wd, gf)


def _gqa_perm():
    return np.concatenate([np.arange(h * HEAD_DIM, (h + 1) * HEAD_DIM) for h in GQA_HEAD_ORDER])


def _layer_weights(lp):
    (norm_attn, w_in, mla_q_norm, w_q_up, mla_kv_norm, w_kv_up, sink_logits, na_rpb,
     group_norm, w_out, norm_mlp, w_mlp_up, w_mlp_down) = lp
    perm = _gqa_perm()
    half = C_ROPE // 2
    z = lambda n: jnp.zeros((D_MODEL, n), w_in.dtype)
    kr = w_in[:, 1664:1696]
    kr_sw = jnp.concatenate([kr[:, half:], kr[:, :half]], axis=1)
    pad = MLA_HEAD_PAD - C_NOPE - C_ROPE
    w_in2 = jnp.concatenate([
        w_in[:, 0:768],
        w_in[:, 768:1024][:, perm], w_in[:, 1024:1280],
        w_in[:, 1696:2464],
        w_in[:, 1280:1664],
        z(C_NOPE), kr, z(pad), z(C_NOPE), kr_sw, z(pad),
    ], axis=1).astype(BF16)

    wq = w_q_up.reshape(C_Q_RANK, C_HEADS, C_NOPE + C_ROPE)
    zq = lambda n: jnp.zeros((C_Q_RANK, C_HEADS, n), w_q_up.dtype)
    rope = wq[:, :, C_NOPE:]
    rope_sw = jnp.concatenate([rope[:, :, half:], rope[:, :, :half]], axis=2)
    wq_main = jnp.concatenate([wq, zq(pad)], axis=2).reshape(C_Q_RANK, -1)
    wq_swap = jnp.concatenate([zq(C_NOPE), rope_sw, zq(pad)], axis=2).reshape(C_Q_RANK, -1)
    wq2 = jnp.concatenate([wq_main, wq_swap], axis=1).astype(BF16)

    wkv = w_kv_up.reshape(C_KV_RANK, C_HEADS, C_NOPE + HEAD_DIM)
    wk2 = jnp.concatenate([wkv[:, :, :C_NOPE], jnp.zeros((C_KV_RANK, C_HEADS, MLA_HEAD_PAD - C_NOPE), w_kv_up.dtype)],
                          axis=2).reshape(C_KV_RANK, -1).astype(BF16)
    wvt = wkv[:, :, C_NOPE:].reshape(C_KV_RANK, -1).T.astype(BF16)

    gn2 = jnp.stack([group_norm[0], group_norm[1][perm], group_norm[2], group_norm[3]])
    w_out2 = jnp.concatenate([w_out[0:256], w_out[256:512][perm], w_out[512:]], axis=0).astype(BF16)
    return dict(
        norm_attn=norm_attn.reshape(1, -1), w_in=w_in2,
        qn=mla_q_norm.reshape(1, -1), kvn=mla_kv_norm.reshape(1, -1), wq=wq2, wk=wk2, wvt=wvt,
        sink=sink_logits[jnp.asarray(GQA_HEAD_ORDER)], rpb=na_rpb,
        gn=gn2, w_out=w_out2, norm_mlp=norm_mlp.reshape(1, -1),
        w_up=w_mlp_up.astype(BF16), w_down=w_mlp_down.astype(BF16))


def _layer(x2, bsz, seq, lw, tables, gf, final):
    ct, st = tables
    qkv_a, qkv_b, qkv_d, c = _inproj(x2, lw["norm_attn"], lw["w_in"])
    a_outs = _mixer_a(qkv_a, bsz, seq)
    ob = _mixer_b(qkv_b, lw["sink"], bsz, seq)
    q, k, vt = _mla_prep(c.reshape(bsz, seq, W_C), ct, st, lw["qn"], lw["kvn"], lw["wq"], lw["wk"], lw["wvt"])
    oc = _mla_flash(q, k, vt).reshape(bsz * seq, GROUP_W)
    od = _mixer_d(qkv_d, lw["rpb"], bsz, seq)
    x2 = _outproj(a_outs, ob, oc, od, x2, lw["gn"], lw["w_out"])
    return _mlp(x2, lw["norm_mlp"], lw["w_up"], lw["w_down"], gf, final)


def _trunk(x, layer_ws, gf):
    bsz, seq, _ = x.shape
    tables = _rope_tables(seq)
    x2 = x.reshape(bsz * seq, D_MODEL)
    for i, lw in enumerate(layer_ws):
        x2 = _layer(x2, bsz, seq, lw, tables, gf, final=(i == len(layer_ws) - 1))
    return x2.reshape(bsz, seq, D_MODEL)


def kernel(x_prompt, x_sample, norm_attn, w_in, mla_q_norm, w_q_up, mla_kv_norm, w_kv_up, sink_logits, na_rpb, group_norm, w_out, norm_mlp, w_mlp_up, w_mlp_down, norm_final):
    params = (norm_attn, w_in, mla_q_norm, w_q_up, mla_kv_norm, w_kv_up, sink_logits, na_rpb,
              group_norm, w_out, norm_mlp, w_mlp_up, w_mlp_down)
    depth = norm_attn.shape[0]
    layer_ws = [_layer_weights(tuple(p[i] for p in params)) for i in range(depth)]
    gf = norm_final.reshape(1, -1)
    return (_trunk(x_prompt, layer_ws, gf), _trunk(x_sample, layer_ws, gf))
```

```python
import functools

import numpy as np
import jax
import jax.numpy as jnp
from jax import lax
from jax.experimental import pallas as pl
from jax.experimental.pallas import tpu as pltpu

F32 = jnp.float32
BF16 = jnp.bfloat16

D_MODEL = 1024
HEAD_DIM = 64
N_GROUPS = 4
GROUP_W = 256
A_CONFIGS = ((128, 1), (512, 4), (2048, 16))
B_HALF_WINDOW = 128
C_Q_RANK = 256
C_KV_RANK = 128
C_NOPE = 64
C_ROPE = 32
C_HEADS = 4
ROPE_THETA = 10000.0
GRID_W = 64
NA_KH = 8
NA_KW = 16
D_FF = 4096
EPS = 1e-5
NEG_INF = -1e30

W_A = 768
W_B = 512
W_D = 768
W_C = 640
MLA_HEAD_PAD = 128
MLA_V_ROWS = 80
LOG2E = 1.4426950408889634

VMEM_LIMIT = 56 * 1024 * 1024


def _cparams(n_axes):
    return pltpu.CompilerParams(dimension_semantics=("arbitrary",) * n_axes,
                                vmem_limit_bytes=VMEM_LIMIT)


def _rms(x, g):
    return x * lax.rsqrt(jnp.mean(x * x, axis=-1, keepdims=True) + EPS) * g


def _inproj_kernel(x_ref, g_ref, w_ref, oa_ref, ob_ref, od_ref, oc_ref):
    h = _rms(x_ref[...], g_ref[...]).astype(BF16)
    off = 0
    for o_ref in (oa_ref, ob_ref, od_ref, oc_ref):
        n = o_ref.shape[-1]
        o_ref[...] = jnp.dot(h, w_ref[:, off:off + n], preferred_element_type=F32).astype(o_ref.dtype)
        off += n


def _inproj(x2, g, w, tm=512):
    t = x2.shape[0]
    widths = (W_A, W_B, W_D, W_C)
    return pl.pallas_call(
        _inproj_kernel,
        grid=(t // tm,),
        in_specs=[pl.BlockSpec((tm, D_MODEL), lambda i: (i, 0)),
                  pl.BlockSpec((1, D_MODEL), lambda i: (0, 0)),
                  pl.BlockSpec((D_MODEL, sum(widths)), lambda i: (0, 0))],
        out_specs=[pl.BlockSpec((tm, n), lambda i: (i, 0)) for n in widths],
        out_shape=[jax.ShapeDtypeStruct((t, n), BF16) for n in widths],
        compiler_params=_cparams(1),
        name="inproj",
    )(x2, g, w)


def _win_attn_kernel(*refs, heads, ck, has_sink, has_lse):
    q_ref, kp_ref, km_ref, kn_ref, vp_ref, vm_ref, vn_ref, bias_ref = refs[:8]
    pos = 8
    sink_ref = None
    if has_sink:
        sink_ref = refs[pos]
        pos += 1
    o_ref = refs[pos]
    lse_ref = refs[pos + 1] if has_lse else None

    q = q_ref[0]
    tq = q.shape[0]
    k = jnp.concatenate([kp_ref[0], km_ref[0], kn_ref[0]], axis=0)
    v = jnp.concatenate([vp_ref[0], vm_ref[0], vn_ref[0]], axis=0)
    lane = lax.broadcasted_iota(jnp.int32, (tq, ck), 1)
    masks = [(lane >= lo) & (lane < lo + HEAD_DIM) for (_, lo) in heads]
    q_stack = jnp.concatenate(
        [jnp.where(mk, q[:, g * ck:(g + 1) * ck], jnp.zeros((), q.dtype))
         for (g, _), mk in zip(heads, masks)], axis=0)
    s = lax.dot_general(q_stack, k, (((1,), (1,)), ((), ())), preferred_element_type=F32)
    s = s * (HEAD_DIM ** -0.5) + bias_ref[0]
    m = jnp.max(s, axis=-1, keepdims=True)
    if has_sink:
        sink = jnp.concatenate([jnp.full((tq, 1), sink_ref[h], F32) for h in range(len(heads))], axis=0)
        m = jnp.maximum(m, sink)
    p = jnp.exp(s - m)
    den = jnp.sum(p, axis=-1, keepdims=True)
    if has_sink:
        den = den + jnp.exp(sink - m)
    pv = jnp.dot(p.astype(v.dtype), v, preferred_element_type=F32) / den
    n_groups = 1 + max(g for g, _ in heads)
    outs = [jnp.zeros((tq, ck), F32) for _ in range(n_groups)]
    for h, ((g, _), mk) in enumerate(zip(heads, masks)):
        outs[g] = jnp.where(mk, pv[h * tq:(h + 1) * tq], outs[g])
    o_ref[0] = jnp.concatenate(outs, axis=1).astype(o_ref.dtype)
    if has_lse:
        lse = m + jnp.log(den)
        louts = [jnp.zeros((tq, ck), F32) for _ in range(n_groups)]
        for h, ((g, _), mk) in enumerate(zip(heads, masks)):
            louts[g] = jnp.where(mk, lse[h * tq:(h + 1) * tq], louts[g])
        lse_ref[0] = jnp.concatenate(louts, axis=1)


def _win_attn(arr, bias, sink, *, n_inner, seq, tq, hb, cq, ck, qcol, kcol, vcol,
              out_cols, heads, has_lse, name):
    b = arr.shape[0]
    nt = seq // tq
    per = tq // hb
    nhb = seq // hb
    nh = len(heads)
    tkw = tq + 2 * hb

    def tile_type(i):
        return jnp.where(i == 0, 0, jnp.where(i == nt - 1, 2, 1))

    def prev_idx(i):
        return jnp.maximum(i * per - 1, 0)

    def next_idx(i):
        return jnp.minimum((i + 1) * per, nhb - 1)

    in_specs = [
        pl.BlockSpec((1, tq, cq), lambda n, r, i: (n, i, qcol(r))),
        pl.BlockSpec((1, hb, ck), lambda n, r, i: (n, prev_idx(i), kcol(r))),
        pl.BlockSpec((1, tq, ck), lambda n, r, i: (n, i, kcol(r))),
        pl.BlockSpec((1, hb, ck), lambda n, r, i: (n, next_idx(i), kcol(r))),
        pl.BlockSpec((1, hb, ck), lambda n, r, i: (n, prev_idx(i), vcol(r))),
        pl.BlockSpec((1, tq, ck), lambda n, r, i: (n, i, vcol(r))),
        pl.BlockSpec((1, hb, ck), lambda n, r, i: (n, next_idx(i), vcol(r))),
        pl.BlockSpec((1, nh * tq, tkw), lambda n, r, i: (tile_type(i), 0, 0)),
    ]
    args = [arr] * 7 + [bias]
    if sink is not None:
        in_specs.append(pl.BlockSpec(memory_space=pltpu.SMEM))
        args.append(sink)
    out_specs = [pl.BlockSpec((1, tq, cq), lambda n, r, i: (n, i, r))]
    out_shape = [jax.ShapeDtypeStruct((b, seq, out_cols), BF16)]
    if has_lse:
        out_specs.append(pl.BlockSpec((1, tq, cq), lambda n, r, i: (n, i, r)))
        out_shape.append(jax.ShapeDtypeStruct((b, seq, out_cols), F32))
    return pl.pallas_call(
        functools.partial(_win_attn_kernel, heads=heads, ck=ck,
                          has_sink=sink is not None, has_lse=has_lse),
        grid=(b, n_inner, nt),
        in_specs=in_specs,
        out_specs=out_specs,
        out_shape=out_shape,
        compiler_params=_cparams(3),
        name=name,
    )(*args)


def _tile_indices(nt):
    return (0, min(1, nt - 1), nt - 1)


def _banded_bias(seq, tq, hb, hw, slopes, dist_scale):
    nt = seq // tq
    tabs = []
    for ti in _tile_indices(nt):
        qpos = ti * tq + np.arange(tq)[:, None]
        kpos = ti * tq - hb + np.arange(tq + 2 * hb)[None, :]
        dist = np.abs(kpos - qpos)
        valid = (dist <= hw) & (kpos >= 0) & (kpos < seq)
        pen = -np.asarray(slopes, np.float32)[:, None, None] * (dist.astype(np.float32) * np.float32(dist_scale))[None]
        tabs.append(np.where(valid[None], pen, np.float32(NEG_INF)).reshape(-1, tq + 2 * hb))
    return jnp.asarray(np.stack(tabs).astype(np.float32))


def _na_bias(rpb, seq, rows_per_tile):
    rows = seq // GRID_W
    kh = min(NA_KH, rows)
    nt = rows // rows_per_tile
    tq = rows_per_tile * GRID_W
    cq = np.arange(GRID_W)[:, None]
    ck = np.arange(GRID_W)[None, :]
    col_start = np.clip(cq - NA_KW // 2, 0, GRID_W - NA_KW)
    col_ok = (ck >= col_start) & (ck < col_start + NA_KW)
    dc = np.clip(ck - cq, -(NA_KW - 1), NA_KW - 1) + (NA_KW - 1)
    col_hot = jnp.asarray(np.eye(2 * NA_KW - 1, dtype=np.float32)[dc])
    tabs = []
    for ti in _tile_indices(nt):
        rq = (ti * rows_per_tile + np.arange(rows_per_tile))[:, None]
        rk = ((ti - 1) * rows_per_tile + np.arange(3 * rows_per_tile))[None, :]
        row_start = np.clip(rq - kh // 2, 0, rows - kh)
        row_ok = (rk >= row_start) & (rk < row_start + kh)
        dr = np.clip(rk - rq + (NA_KH - 1), 0, 2 * NA_KH - 2)
        row_hot = jnp.asarray(np.eye(2 * NA_KH - 1, dtype=np.float32)[dr])
        vals = jnp.einsum("abr,hrc,xyc->haxby", row_hot, rpb.astype(F32), col_hot,
                          precision=lax.Precision.HIGHEST)
        valid = row_ok[:, None, :, None] & col_ok[None, :, None, :]
        tabs.append(jnp.where(valid[None], vals, NEG_INF).reshape(-1, 3 * tq))
    return jnp.stack(tabs)


def _alibi_slopes():
    s = np.exp2(-8.0 * np.arange(1, 9, dtype=np.float32) / 8.0)
    return s[1::2], s[0::2]


HEADS_FULL = ((0, 0), (0, 64), (0, 128), (0, 192))
HEADS_GQA = ((0, 0), (0, 64), (1, 0), (1, 64))
GQA_HEAD_ORDER = (0, 2, 1, 3)


def _mixer_a(qkv_a, bsz, seq):
    slopes_a, _ = _alibi_slopes()
    outs = []
    for window, dil in A_CONFIGS:
        length = seq // dil
        hw = window // (2 * dil)
        tq = min(128, length)
        bias = _banded_bias(length, tq, hw, hw, slopes_a, float(dil))
        o, lse = _win_attn(
            qkv_a.reshape(bsz, length, dil * W_A), bias, None,
            n_inner=dil, seq=length, tq=tq, hb=hw, cq=GROUP_W, ck=GROUP_W,
            qcol=lambda r: 3 * r, kcol=lambda r: 3 * r + 1, vcol=lambda r: 3 * r + 2,
            out_cols=dil * GROUP_W, heads=HEADS_FULL, has_lse=True, name=f"mix_a_d{dil}")
        outs.append((o.reshape(bsz * seq, GROUP_W), lse.reshape(bsz * seq, GROUP_W)))
    return outs


def _mixer_b(qkv_b, sink, bsz, seq):
    _, slopes_b = _alibi_slopes()
    tq = 256
    bias = _banded_bias(seq, tq, B_HALF_WINDOW, B_HALF_WINDOW, slopes_b[list(GQA_HEAD_ORDER)], 1.0)
    (o,) = _win_attn(
        qkv_b.reshape(bsz, seq, W_B), bias, sink,
        n_inner=1, seq=seq, tq=tq, hb=B_HALF_WINDOW, cq=GROUP_W, ck=128,
        qcol=lambda r: 0, kcol=lambda r: 2, vcol=lambda r: 3,
        out_cols=GROUP_W, heads=HEADS_GQA, has_lse=False, name="mix_b")
    return o.reshape(bsz * seq, GROUP_W)


def _mixer_d(qkv_d, rpb, bsz, seq):
    rows_per_tile = 4
    tq = rows_per_tile * GRID_W
    bias = _na_bias(rpb, seq, rows_per_tile)
    (o,) = _win_attn(
        qkv_d.reshape(bsz, seq, W_D), bias, None,
        n_inner=1, seq=seq, tq=tq, hb=tq, cq=GROUP_W, ck=GROUP_W,
        qcol=lambda r: 0, kcol=lambda r: 1, vcol=lambda r: 2,
        out_cols=GROUP_W, heads=HEADS_FULL, has_lse=False, name="mix_d")
    return o.reshape(bsz * seq, GROUP_W)


def _mla_prep_kernel(c_ref, ct_ref, st_ref, qn_ref, kvn_ref, wq_ref, wk_ref, wvt_ref,
                     q_ref, k_ref, vt_ref):
    hp = MLA_HEAD_PAD
    c = c_ref[0]
    ct = ct_ref[...]
    st = st_ref[...]
    cqn = _rms(c[:, :C_Q_RANK].astype(F32), qn_ref[...]).astype(BF16)
    qq = jnp.dot(cqn, wq_ref[...], preferred_element_type=F32)
    scale = (C_NOPE + C_ROPE) ** -0.5 * LOG2E
    q_ref[0] = jnp.concatenate(
        [(qq[:, h * hp:(h + 1) * hp] * ct + qq[:, (C_HEADS + h) * hp:(C_HEADS + h + 1) * hp] * st) * scale
         for h in range(C_HEADS)], axis=1).astype(q_ref.dtype)
    ckvn = _rms(c[:, C_Q_RANK:C_Q_RANK + C_KV_RANK].astype(F32), kvn_ref[...]).astype(BF16)
    kk = jnp.dot(ckvn, wk_ref[...], preferred_element_type=F32)
    o0 = C_Q_RANK + C_KV_RANK
    kpe = c[:, o0:o0 + hp].astype(F32) * ct + c[:, o0 + hp:o0 + 2 * hp].astype(F32) * st
    k_ref[0] = jnp.concatenate([kk[:, h * hp:(h + 1) * hp] + kpe for h in range(C_HEADS)],
                               axis=1).astype(k_ref.dtype)
    vt = lax.dot_general(wvt_ref[...], ckvn, (((1,), (1,)), ((), ())), preferred_element_type=F32)
    ones = jnp.ones((MLA_V_ROWS - HEAD_DIM, vt.shape[1]), F32)
    vt_ref[0] = jnp.concatenate(
        [blk for h in range(C_HEADS) for blk in (vt[h * HEAD_DIM:(h + 1) * HEAD_DIM], ones)],
        axis=0).astype(vt_ref.dtype)


def _mla_prep(c3, ct, st, qn, kvn, wq, wk, wvt, tm=512):
    b, s, _ = c3.shape
    hw = C_HEADS * MLA_HEAD_PAD
    const = lambda shape: pl.BlockSpec(shape, lambda n, i: (0,) * len(shape))
    return pl.pallas_call(
        _mla_prep_kernel,
        grid=(b, s // tm),
        in_specs=[pl.BlockSpec((1, tm, W_C), lambda n, i: (n, i, 0)),
                  pl.BlockSpec((tm, MLA_HEAD_PAD), lambda n, i: (i, 0)),
                  pl.BlockSpec((tm, MLA_HEAD_PAD), lambda n, i: (i, 0)),
                  const((1, C_Q_RANK)), const((1, C_KV_RANK)),
                  const((C_Q_RANK, 2 * hw)), const((C_KV_RANK, hw)), const((GROUP_W, C_KV_RANK))],
        out_specs=[pl.BlockSpec((1, tm, hw), lambda n, i: (n, i, 0)),
                   pl.BlockSpec((1, tm, hw), lambda n, i: (n, i, 0)),
                   pl.BlockSpec((1, C_HEADS * MLA_V_ROWS, tm), lambda n, i: (n, 0, i))],
        out_shape=[jax.ShapeDtypeStruct((b, s, hw), BF16),
                   jax.ShapeDtypeStruct((b, s, hw), BF16),
                   jax.ShapeDtypeStruct((b, C_HEADS * MLA_V_ROWS, s), BF16)],
        compiler_params=_cparams(2),
        name="mla_prep",
    )(c3, ct, st, qn, kvn, wq, wk, wvt)


def _mla_flash_kernel(q_ref, k_ref, vt_ref, o_ref, m_sc, acc_sc, s_sc, *, kc):
    hp, hv = MLA_HEAD_PAD, MLA_V_ROWS
    kv = pl.program_id(2)
    tk = k_ref.shape[1]

    @pl.when(kv == 0)
    def _():
        m_sc[...] = jnp.full(m_sc.shape, NEG_INF, F32)
        acc_sc[...] = jnp.zeros(acc_sc.shape, F32)

    units = [(c, h) for c in range(tk // kc) for h in range(C_HEADS)]

    def scores(u):
        c, h = units[u]
        s_sc[u % 2] = lax.dot_general(k_ref[0, c * kc:(c + 1) * kc, h * hp:(h + 1) * hp],
                                      q_ref[0, :, h * hp:(h + 1) * hp],
                                      (((1,), (1,)), ((), ())), preferred_element_type=F32)

    scores(0)
    for u, (c, h) in enumerate(units):
        if u + 1 < len(units):
            scores(u + 1)
        st = s_sc[u % 2]
        m_old = m_sc[h:h + 1, :]
        m_new = jnp.maximum(m_old, jnp.max(st, axis=0, keepdims=True))
        alpha = jnp.exp2(m_old - m_new)
        pt = jnp.exp2(st - m_new).astype(BF16)
        m_sc[h:h + 1, :] = m_new
        rows = slice(h * hv, (h + 1) * hv)
        acc_sc[rows, :] = alpha * acc_sc[rows, :] + jnp.dot(
            vt_ref[0, rows, c * kc:(c + 1) * kc], pt, preferred_element_type=F32)

    @pl.when(kv == pl.num_programs(2) - 1)
    def _():
        ot = jnp.concatenate(
            [acc_sc[h * hv:h * hv + HEAD_DIM, :] / acc_sc[h * hv + HEAD_DIM:h * hv + HEAD_DIM + 1, :]
             for h in range(C_HEADS)], axis=0)
        o_ref[0] = ot.T.astype(o_ref.dtype)


def _mla_flash(q, k, vt, tq=512, tk=2048, kc=512):
    b, s, hw = q.shape
    tq = min(tq, s)
    tk = min(tk, s)
    vrows = C_HEADS * MLA_V_ROWS
    return pl.pallas_call(
        functools.partial(_mla_flash_kernel, kc=kc),
        grid=(b, s // tq, s // tk),
        in_specs=[pl.BlockSpec((1, tq, hw), lambda n, i, j: (n, i, 0)),
                  pl.BlockSpec((1, tk, hw), lambda n, i, j: (n, j, 0)),
                  pl.BlockSpec((1, vrows, tk), lambda n, i, j: (n, 0, j))],
        out_specs=pl.BlockSpec((1, tq, GROUP_W), lambda n, i, j: (n, i, 0)),
        out_shape=jax.ShapeDtypeStruct((b, s, GROUP_W), BF16),
        scratch_shapes=[pltpu.VMEM((8, tq), F32), pltpu.VMEM((vrows, tq), F32),
                        pltpu.VMEM((2, kc, tq), F32)],
        compiler_params=_cparams(3),
        name="mla_flash",
    )(q, k, vt)


def _rope_tables(seq):
    inv_freq = 1.0 / (ROPE_THETA ** (jnp.arange(0, C_ROPE, 2, dtype=F32) / C_ROPE))
    ang = jnp.arange(seq, dtype=F32)[:, None] * inv_freq[None, :]
    cos, sin = jnp.cos(ang), jnp.sin(ang)
    pad = MLA_HEAD_PAD - C_NOPE - C_ROPE
    ct = jnp.concatenate([jnp.ones((seq, C_NOPE), F32), cos, cos, jnp.zeros((seq, pad), F32)], axis=1)
    st = jnp.concatenate([jnp.zeros((seq, C_NOPE), F32), -sin, sin, jnp.zeros((seq, pad), F32)], axis=1)
    return ct, st


def _outproj_kernel(a1_ref, a2_ref, a3_ref, l1_ref, l2_ref, l3_ref, ob_ref, oc_ref, od_ref,
                    x_ref, g_ref, w_ref, o_ref):
    ls = [l1_ref[...], l2_ref[...], l3_ref[...]]
    m = jnp.maximum(jnp.maximum(ls[0], ls[1]), ls[2])
    es = [jnp.exp(l - m) for l in ls]
    den = es[0] + es[1] + es[2]
    oa = sum((e / den) * a_ref[...].astype(F32) for e, a_ref in zip(es, (a1_ref, a2_ref, a3_ref)))
    groups = [oa, ob_ref[...].astype(F32), oc_ref[...].astype(F32), od_ref[...].astype(F32)]
    y = jnp.concatenate([_rms(o, g_ref[i:i + 1, :]).astype(BF16) for i, o in enumerate(groups)], axis=1)
    o_ref[...] = x_ref[...] + jnp.dot(y, w_ref[...], preferred_element_type=F32)


def _outproj(a_outs, ob, oc, od, x2, g, w, tm=512):
    t = x2.shape[0]
    tok = lambda n: pl.BlockSpec((tm, n), lambda i: (i, 0))
    args = [o for o, _ in a_outs] + [l for _, l in a_outs] + [ob, oc, od, x2, g, w]
    return pl.pallas_call(
        _outproj_kernel,
        grid=(t // tm,),
        in_specs=[tok(GROUP_W)] * 9 + [tok(D_MODEL),
                                       pl.BlockSpec((N_GROUPS, GROUP_W), lambda i: (0, 0)),
                                       pl.BlockSpec((D_MODEL, D_MODEL), lambda i: (0, 0))],
        out_specs=tok(D_MODEL),
        out_shape=jax.ShapeDtypeStruct((t, D_MODEL), F32),
        compiler_params=_cparams(1),
        name="outproj",
    )(*args)


def _mlp_kernel(x_ref, g_ref, wu_ref, wd_ref, gf_ref, o_ref, *, final, ff_chunk):
    x = x_ref[...]
    h = _rms(x, g_ref[...]).astype(BF16)
    acc = x
    for c in range(D_FF // ff_chunk):
        cols = slice(c * ff_chunk, (c + 1) * ff_chunk)
        u = jnp.dot(h, wu_ref[:, cols], preferred_element_type=F32)
        u = jnp.square(jnp.maximum(u, 0.0)).astype(BF16)
        acc = acc + jnp.dot(u, wd_ref[cols, :], preferred_element_type=F32)
    if final:
        acc = _rms(acc, gf_ref[...])
    o_ref[...] = acc


def _mlp(x2, g, wu, wd, gf, final, tm=512, ff_chunk=1024):
    t = x2.shape[0]
    return pl.pallas_call(
        functools.partial(_mlp_kernel, final=final, ff_chunk=ff_chunk),
        grid=(t // tm,),
        in_specs=[pl.BlockSpec((tm, D_MODEL), lambda i: (i, 0)),
                  pl.BlockSpec((1, D_MODEL), lambda i: (0, 0)),
                  pl.BlockSpec((D_MODEL, D_FF), lambda i: (0, 0)),
                  pl.BlockSpec((D_FF, D_MODEL), lambda i: (0, 0)),
                  pl.BlockSpec((1, D_MODEL), lambda i: (0, 0))],
        out_specs=pl.BlockSpec((tm, D_MODEL), lambda i: (i, 0)),
        out_shape=jax.ShapeDtypeStruct((t, D_MODEL), F32),
        compiler_params=_cparams(1),
        name="mlp_final" if final else "mlp",
    )(x2, g, wu, wd, gf)


def _gqa_perm():
    return np.concatenate([np.arange(h * HEAD_DIM, (h + 1) * HEAD_DIM) for h in GQA_HEAD_ORDER])


def _layer_weights(lp):
    (norm_attn, w_in, mla_q_norm, w_q_up, mla_kv_norm, w_kv_up, sink_logits, na_rpb,
     group_norm, w_out, norm_mlp, w_mlp_up, w_mlp_down) = lp
    perm = _gqa_perm()
    half = C_ROPE // 2
    z = lambda n: jnp.zeros((D_MODEL, n), w_in.dtype)
    kr = w_in[:, 1664:1696]
    kr_sw = jnp.concatenate([kr[:, half:], kr[:, :half]], axis=1)
    pad = MLA_HEAD_PAD - C_NOPE - C_ROPE
    w_in2 = jnp.concatenate([
        w_in[:, 0:768],
        w_in[:, 768:1024][:, perm], w_in[:, 1024:1280],
        w_in[:, 1696:2464],
        w_in[:, 1280:1664],
        z(C_NOPE), kr, z(pad), z(C_NOPE), kr_sw, z(pad),
    ], axis=1).astype(BF16)

    wq = w_q_up.reshape(C_Q_RANK, C_HEADS, C_NOPE + C_ROPE)
    zq = lambda n: jnp.zeros((C_Q_RANK, C_HEADS, n), w_q_up.dtype)
    rope = wq[:, :, C_NOPE:]
    rope_sw = jnp.concatenate([rope[:, :, half:], rope[:, :, :half]], axis=2)
    wq_main = jnp.concatenate([wq, zq(pad)], axis=2).reshape(C_Q_RANK, -1)
    wq_swap = jnp.concatenate([zq(C_NOPE), rope_sw, zq(pad)], axis=2).reshape(C_Q_RANK, -1)
    wq2 = jnp.concatenate([wq_main, wq_swap], axis=1).astype(BF16)

    wkv = w_kv_up.reshape(C_KV_RANK, C_HEADS, C_NOPE + HEAD_DIM)
    wk2 = jnp.concatenate([wkv[:, :, :C_NOPE], jnp.zeros((C_KV_RANK, C_HEADS, MLA_HEAD_PAD - C_NOPE), w_kv_up.dtype)],
                          axis=2).reshape(C_KV_RANK, -1).astype(BF16)
    wvt = wkv[:, :, C_NOPE:].reshape(C_KV_RANK, -1).T.astype(BF16)

    gn2 = jnp.stack([group_norm[0], group_norm[1][perm], group_norm[2], group_norm[3]])
    w_out2 = jnp.concatenate([w_out[0:256], w_out[256:512][perm], w_out[512:]], axis=0).astype(BF16)
    return dict(
        norm_attn=norm_attn.reshape(1, -1), w_in=w_in2,
        qn=mla_q_norm.reshape(1, -1), kvn=mla_kv_norm.reshape(1, -1), wq=wq2, wk=wk2, wvt=wvt,
        sink=sink_logits[jnp.asarray(GQA_HEAD_ORDER)], rpb=na_rpb,
        gn=gn2, w_out=w_out2, norm_mlp=norm_mlp.reshape(1, -1),
        w_up=w_mlp_up.astype(BF16), w_down=w_mlp_down.astype(BF16))


def _layer(x2, bsz, seq, lw, tables, gf, final):
    ct, st = tables
    qkv_a, qkv_b, qkv_d, c = _inproj(x2, lw["norm_attn"], lw["w_in"])
    a_outs = _mixer_a(qkv_a, bsz, seq)
    ob = _mixer_b(qkv_b, lw["sink"], bsz, seq)
    q, k, vt = _mla_prep(c.reshape(bsz, seq, W_C), ct, st, lw["qn"], lw["kvn"], lw["wq"], lw["wk"], lw["wvt"])
    oc = _mla_flash(q, k, vt).reshape(bsz * seq, GROUP_W)
    od = _mixer_d(qkv_d, lw["rpb"], bsz, seq)
    x2 = _outproj(a_outs, ob, oc, od, x2, lw["gn"], lw["w_out"])
    return _mlp(x2, lw["norm_mlp"], lw["w_up"], lw["w_down"], gf, final)


def _trunk(x, layer_ws, gf):
    bsz, seq, _ = x.shape
    tables = _rope_tables(seq)
    x2 = x.reshape(bsz * seq, D_MODEL)
    for i, lw in enumerate(layer_ws):
        x2 = _layer(x2, bsz, seq, lw, tables, gf, final=(i == len(layer_ws) - 1))
    return x2.reshape(bsz, seq, D_MODEL)


def kernel(x_prompt, x_sample, norm_attn, w_in, mla_q_norm, w_q_up, mla_kv_norm, w_kv_up, sink_logits, na_rpb, group_norm, w_out, norm_mlp, w_mlp_up, w_mlp_down, norm_final):
    params = (norm_attn, w_in, mla_q_norm, w_q_up, mla_kv_norm, w_kv_up, sink_logits, na_rpb,
              group_norm, w_out, norm_mlp, w_mlp_up, w_mlp_down)
    depth = norm_attn.shape[0]
    layer_ws = [_layer_weights(tuple(p[i] for p in params)) for i in range(depth)]
    gf = norm_final.reshape(1, -1)
    return (_trunk(x_prompt, layer_ws, gf), _trunk(x_sample, layer_ws, gf))
```

```python
import functools

import numpy as np
import jax
import jax.numpy as jnp
from jax import lax
from jax.experimental import pallas as pl
from jax.experimental.pallas import tpu as pltpu

F32 = jnp.float32
BF16 = jnp.bfloat16

D_MODEL = 1024
HEAD_DIM = 64
N_GROUPS = 4
GROUP_W = 256
A_CONFIGS = ((128, 1), (512, 4), (2048, 16))
B_HALF_WINDOW = 128
C_Q_RANK = 256
C_KV_RANK = 128
C_NOPE = 64
C_ROPE = 32
C_HEADS = 4
ROPE_THETA = 10000.0
GRID_W = 64
NA_KH = 8
NA_KW = 16
D_FF = 4096
EPS = 1e-5
NEG_INF = -1e30

W_A = 768
W_B = 384
W_D = 512
W_VT = 384
W_C = 640
MLA_HEAD_PAD = 128
MLA_V_ROWS = 80
LOG2E = 1.4426950408889634

VMEM_LIMIT = 56 * 1024 * 1024


def _cparams(n_axes):
    return pltpu.CompilerParams(dimension_semantics=("arbitrary",) * n_axes,
                                vmem_limit_bytes=VMEM_LIMIT)


def _rms(x, g):
    return x * lax.rsqrt(jnp.mean(x * x, axis=-1, keepdims=True) + EPS) * g


def _inproj_kernel(x_ref, g_ref, w_ref, wvt_ref, oa_ref, ob_ref, od_ref, oc_ref, vtb_ref, vtd_ref):
    h = _rms(x_ref[...], g_ref[...]).astype(BF16)
    off = 0
    for o_ref in (oa_ref, ob_ref, od_ref, oc_ref):
        n = o_ref.shape[-1]
        o_ref[...] = jnp.dot(h, w_ref[:, off:off + n], preferred_element_type=F32).astype(o_ref.dtype)
        off += n
    vt = lax.dot_general(wvt_ref[...], h, (((1,), (1,)), ((), ())), preferred_element_type=F32)
    nb = vtb_ref.shape[0]
    vtb_ref[...] = vt[:nb].astype(vtb_ref.dtype)
    vtd_ref[...] = vt[nb:].astype(vtd_ref.dtype)


def _inproj(x2, g, w, wvt, tm=512):
    t = x2.shape[0]
    widths = (W_A, W_B, W_D, W_C)
    vrows = (W_VT - GROUP_W, GROUP_W)
    return pl.pallas_call(
        _inproj_kernel,
        grid=(t // tm,),
        in_specs=[pl.BlockSpec((tm, D_MODEL), lambda i: (i, 0)),
                  pl.BlockSpec((1, D_MODEL), lambda i: (0, 0)),
                  pl.BlockSpec((D_MODEL, sum(widths)), lambda i: (0, 0)),
                  pl.BlockSpec((W_VT, D_MODEL), lambda i: (0, 0))],
        out_specs=([pl.BlockSpec((tm, n), lambda i: (i, 0)) for n in widths]
                   + [pl.BlockSpec((r, tm), lambda i: (0, i)) for r in vrows]),
        out_shape=([jax.ShapeDtypeStruct((t, n), BF16) for n in widths]
                   + [jax.ShapeDtypeStruct((r, t), BF16) for r in vrows]),
        compiler_params=_cparams(1),
        name="inproj",
    )(x2, g, w, wvt)


def _win_attn_t_kernel(*refs, heads, ck, tq, hb, n_tiles, has_sink, has_lse, v_rows):
    q_ref, kp_ref, km_ref, kn_ref, vp_ref, vm_ref, vn_ref, bias_ref = refs[:8]
    sink_ref = refs[8] if has_sink else None
    o_ref = refs[-2] if has_lse else refs[-1]
    lse_ref = refs[-1] if has_lse else None
    ts = q_ref.shape[0]
    tkw = tq + 2 * hb
    nh = len(heads)
    kw = jnp.concatenate([kp_ref[...], km_ref[...], kn_ref[...]], axis=0)
    if v_rows:
        vw = jnp.concatenate([vp_ref[...], vm_ref[...], vn_ref[...]], axis=0).astype(F32).T.astype(BF16)
    else:
        vw = jnp.concatenate([vp_ref[...], vm_ref[...], vn_ref[...]], axis=1)
    lane = lax.broadcasted_iota(jnp.int32, (tq, ck), 1)
    masks = [(lane >= lo) & (lane < lo + HEAD_DIM) for (_, lo, _, _) in heads]
    ones = jnp.ones((MLA_V_ROWS - HEAD_DIM, tkw), vw.dtype)
    step = pl.program_id(2)
    for j in range(ts // tq):
        tile = step * (ts // tq) + j
        kind = jnp.where(tile == 0, 0, jnp.where(tile == n_tiles - 1, 2, 1))
        q = q_ref[j * tq:(j + 1) * tq, :]
        q_stack = jnp.concatenate(
            [jnp.where(mk, q[:, g * ck:(g + 1) * ck], jnp.zeros((), q.dtype))
             for (g, _, _, _), mk in zip(heads, masks)], axis=0)
        st = lax.dot_general(kw[j * tq:j * tq + tkw], q_stack, (((1,), (1,)), ((), ())),
                             preferred_element_type=F32) + bias_ref[kind]
        m = jnp.max(st, axis=0, keepdims=True)
        if has_sink:
            sink = jnp.concatenate([jnp.full((1, tq), sink_ref[i] * LOG2E, F32) for i in range(nh)], axis=1)
            m = jnp.maximum(m, sink)
            sink_p = jnp.exp2(sink - m)
        pt = jnp.exp2(st - m).astype(BF16)
        outs = [None] * nh
        lses = [None] * nh
        for i, (_, _, vrow, out_pos) in enumerate(heads):
            lhs = jnp.concatenate([vw[vrow:vrow + HEAD_DIM, j * tq:j * tq + tkw], ones], axis=0)
            r = jnp.dot(lhs, pt[:, i * tq:(i + 1) * tq], preferred_element_type=F32)
            den = r[HEAD_DIM:HEAD_DIM + 1]
            if has_sink:
                den = den + sink_p[:, i * tq:(i + 1) * tq]
            outs[out_pos] = r[:HEAD_DIM] / den
            if has_lse:
                lse = (m[:, i * tq:(i + 1) * tq] + jnp.log2(den)) * (1.0 / LOG2E)
                lses[out_pos] = jnp.broadcast_to(lse, (HEAD_DIM, tq))
        o_ref[j * tq:(j + 1) * tq, :] = jnp.concatenate(outs, axis=0).T.astype(o_ref.dtype)
        if has_lse:
            lse_ref[j * tq:(j + 1) * tq, :] = jnp.concatenate(lses, axis=0).T


def _win_attn_t(qk, vt, bias, sink, *, bsz, n_inner, seq, ts, tq, hb, cq, ck, qcol, kcol, vcol, ocol,
                out_cols, heads, has_lse, name):
    ts = min(ts, seq)
    tq = min(tq, ts)
    nsteps = seq // ts
    per = ts // hb
    nhb = seq // hb
    nh = len(heads)
    tkw = tq + 2 * hb
    v_rows = vt is None

    def prev_idx(n, i):
        return n * nhb + jnp.maximum(i * per - 1, 0)

    def next_idx(n, i):
        return n * nhb + jnp.minimum((i + 1) * per, nhb - 1)

    in_specs = [
        pl.BlockSpec((ts, cq), lambda n, r, i: (n * nsteps + i, qcol(r))),
        pl.BlockSpec((hb, ck), lambda n, r, i: (prev_idx(n, i), kcol(r))),
        pl.BlockSpec((ts, ck), lambda n, r, i: (n * nsteps + i, kcol(r))),
        pl.BlockSpec((hb, ck), lambda n, r, i: (next_idx(n, i), kcol(r))),
    ]
    if v_rows:
        in_specs += [
            pl.BlockSpec((hb, GROUP_W), lambda n, r, i: (prev_idx(n, i), vcol(r))),
            pl.BlockSpec((ts, GROUP_W), lambda n, r, i: (n * nsteps + i, vcol(r))),
            pl.BlockSpec((hb, GROUP_W), lambda n, r, i: (next_idx(n, i), vcol(r))),
        ]
        args = [qk] * 7
    else:
        cv = vt.shape[0]
        in_specs += [
            pl.BlockSpec((cv, hb), lambda n, r, i: (0, prev_idx(n, i))),
            pl.BlockSpec((cv, ts), lambda n, r, i: (0, n * nsteps + i)),
            pl.BlockSpec((cv, hb), lambda n, r, i: (0, next_idx(n, i))),
        ]
        args = [qk] * 4 + [vt] * 3
    in_specs.append(pl.BlockSpec((3, tkw, nh * tq), lambda n, r, i: (0, 0, 0)))
    args.append(bias)
    if sink is not None:
        in_specs.append(pl.BlockSpec(memory_space=pltpu.SMEM))
        args.append(sink)
    out_specs = [pl.BlockSpec((ts, GROUP_W), lambda n, r, i: (n * nsteps + i, ocol(r)))]
    out_shape = [jax.ShapeDtypeStruct((bsz * seq, out_cols), BF16)]
    if has_lse:
        out_specs.append(pl.BlockSpec((ts, GROUP_W), lambda n, r, i: (n * nsteps + i, ocol(r))))
        out_shape.append(jax.ShapeDtypeStruct((bsz * seq, out_cols), F32))
    return pl.pallas_call(
        functools.partial(_win_attn_t_kernel, heads=heads, ck=ck, tq=tq, hb=hb, n_tiles=seq // tq,
                          has_sink=sink is not None, has_lse=has_lse, v_rows=v_rows),
        grid=(bsz, n_inner, nsteps),
        in_specs=in_specs,
        out_specs=out_specs,
        out_shape=out_shape,
        compiler_params=_cparams(3),
        name=name,
    )(*args)


def _transpose_bias(bias):
    return jnp.swapaxes(bias, 1, 2) * LOG2E


def _tile_indices(nt):
    return (0, min(1, nt - 1), nt - 1)


def _banded_bias(seq, tq, hb, hw, slopes, dist_scale):
    nt = seq // tq
    tabs = []
    for ti in _tile_indices(nt):
        qpos = ti * tq + np.arange(tq)[:, None]
        kpos = ti * tq - hb + np.arange(tq + 2 * hb)[None, :]
        dist = np.abs(kpos - qpos)
        valid = (dist <= hw) & (kpos >= 0) & (kpos < seq)
        pen = -np.asarray(slopes, np.float32)[:, None, None] * (dist.astype(np.float32) * np.float32(dist_scale))[None]
        tabs.append(np.where(valid[None], pen, np.float32(NEG_INF)).reshape(-1, tq + 2 * hb))
    return jnp.asarray(np.stack(tabs).astype(np.float32))


def _na_bias(rpb, seq, rows_per_tile):
    rows = seq // GRID_W
    kh = min(NA_KH, rows)
    nt = rows // rows_per_tile
    tq = rows_per_tile * GRID_W
    cq = np.arange(GRID_W)[:, None]
    ck = np.arange(GRID_W)[None, :]
    col_start = np.clip(cq - NA_KW // 2, 0, GRID_W - NA_KW)
    col_ok = (ck >= col_start) & (ck < col_start + NA_KW)
    dc = np.clip(ck - cq, -(NA_KW - 1), NA_KW - 1) + (NA_KW - 1)
    col_hot = jnp.asarray(np.eye(2 * NA_KW - 1, dtype=np.float32)[dc])
    tabs = []
    for ti in _tile_indices(nt):
        rq = (ti * rows_per_tile + np.arange(rows_per_tile))[:, None]
        rk = ((ti - 1) * rows_per_tile + np.arange(3 * rows_per_tile))[None, :]
        row_start = np.clip(rq - kh // 2, 0, rows - kh)
        row_ok = (rk >= row_start) & (rk < row_start + kh)
        dr = np.clip(rk - rq + (NA_KH - 1), 0, 2 * NA_KH - 2)
        row_hot = jnp.asarray(np.eye(2 * NA_KH - 1, dtype=np.float32)[dr])
        vals = jnp.einsum("abr,hrc,xyc->haxby", row_hot, rpb.astype(F32), col_hot,
                          precision=lax.Precision.HIGHEST)
        valid = row_ok[:, None, :, None] & col_ok[None, :, None, :]
        tabs.append(jnp.where(valid[None], vals, NEG_INF).reshape(-1, 3 * tq))
    return jnp.stack(tabs)


def _alibi_slopes():
    s = np.exp2(-8.0 * np.arange(1, 9, dtype=np.float32) / 8.0)
    return s[1::2], s[0::2]


GQA_HEAD_ORDER = (0, 2, 1, 3)
HEADS_T_FULL = tuple((0, 64 * h, 64 * h, h) for h in range(4))
HEADS_T_GQA = ((0, 0, 0, 0), (0, 64, 64, 2), (1, 0, 0, 1), (1, 64, 64, 3))

A_HALO = 128


def _mixer_a(qkv_a, bsz, seq):
    slopes_a, _ = _alibi_slopes()
    outs = []
    for window, dil in A_CONFIGS:
        length = seq // dil
        hw = window // (2 * dil)
        tq = min(256, length)
        bias = _banded_bias(length, tq, A_HALO, hw, slopes_a, float(dil))
        o, lse = _win_attn_t(
            qkv_a.reshape(bsz * length, dil * W_A), None, _transpose_bias(bias), None,
            bsz=bsz, n_inner=dil, seq=length, ts=1024, tq=tq, hb=A_HALO, cq=GROUP_W, ck=GROUP_W,
            qcol=lambda r: 3 * r, kcol=lambda r: 3 * r + 1, vcol=lambda r: 3 * r + 2, ocol=lambda r: r,
            out_cols=dil * GROUP_W, heads=HEADS_T_FULL, has_lse=True, name=f"mix_a_d{dil}")
        outs.append((o.reshape(bsz * seq, GROUP_W), lse.reshape(bsz * seq, GROUP_W)))
    return outs


def _mixer_b(qk_b, vt_b, sink, bsz, seq):
    _, slopes_b = _alibi_slopes()
    tq = 256
    bias = _banded_bias(seq, tq, B_HALF_WINDOW, B_HALF_WINDOW, slopes_b[list(GQA_HEAD_ORDER)], 1.0)
    (o,) = _win_attn_t(
        qk_b, vt_b, _transpose_bias(bias), sink, bsz=bsz, n_inner=1, seq=seq, ts=1024, tq=tq,
        hb=B_HALF_WINDOW, cq=GROUP_W, ck=128, qcol=lambda r: 0, kcol=lambda r: 2, vcol=None, ocol=lambda r: 0,
        out_cols=GROUP_W, heads=HEADS_T_GQA, has_lse=False, name="mix_b")
    return o


def _mixer_d(qk_d, vt_d, rpb, bsz, seq):
    rows_per_tile = 4
    tq = rows_per_tile * GRID_W
    bias = _na_bias(rpb, seq, rows_per_tile)
    (o,) = _win_attn_t(
        qk_d, vt_d, _transpose_bias(bias), None, bsz=bsz, n_inner=1, seq=seq, ts=1024, tq=tq,
        hb=tq, cq=GROUP_W, ck=GROUP_W, qcol=lambda r: 0, kcol=lambda r: 1, vcol=None, ocol=lambda r: 0,
        out_cols=GROUP_W, heads=HEADS_T_FULL, has_lse=False, name="mix_d")
    return o


def _mla_prep_kernel(c_ref, ct_ref, st_ref, qn_ref, kvn_ref, wq_ref, wk_ref, wvt_ref,
                     q_ref, k_ref, vt_ref):
    hp = MLA_HEAD_PAD
    c = c_ref[0]
    ct = ct_ref[...]
    st = st_ref[...]
    cqn = _rms(c[:, :C_Q_RANK].astype(F32), qn_ref[...]).astype(BF16)
    qq = jnp.dot(cqn, wq_ref[...], preferred_element_type=F32)
    scale = (C_NOPE + C_ROPE) ** -0.5 * LOG2E
    q_ref[0] = jnp.concatenate(
        [(qq[:, h * hp:(h + 1) * hp] * ct + qq[:, (C_HEADS + h) * hp:(C_HEADS + h + 1) * hp] * st) * scale
         for h in range(C_HEADS)], axis=1).astype(q_ref.dtype)
    ckvn = _rms(c[:, C_Q_RANK:C_Q_RANK + C_KV_RANK].astype(F32), kvn_ref[...]).astype(BF16)
    kk = jnp.dot(ckvn, wk_ref[...], preferred_element_type=F32)
    o0 = C_Q_RANK + C_KV_RANK
    kpe = c[:, o0:o0 + hp].astype(F32) * ct + c[:, o0 + hp:o0 + 2 * hp].astype(F32) * st
    k_ref[0] = jnp.concatenate([kk[:, h * hp:(h + 1) * hp] + kpe for h in range(C_HEADS)],
                               axis=1).astype(k_ref.dtype)
    vt = lax.dot_general(wvt_ref[...], ckvn, (((1,), (1,)), ((), ())), preferred_element_type=F32)
    ones = jnp.ones((MLA_V_ROWS - HEAD_DIM, vt.shape[1]), F32)
    vt_ref[0] = jnp.concatenate(
        [blk for h in range(C_HEADS) for blk in (vt[h * HEAD_DIM:(h + 1) * HEAD_DIM], ones)],
        axis=0).astype(vt_ref.dtype)


def _mla_prep(c3, ct, st, qn, kvn, wq, wk, wvt, tm=512):
    b, s, _ = c3.shape
    hw = C_HEADS * MLA_HEAD_PAD
    const = lambda shape: pl.BlockSpec(shape, lambda n, i: (0,) * len(shape))
    return pl.pallas_call(
        _mla_prep_kernel,
        grid=(b, s // tm),
        in_specs=[pl.BlockSpec((1, tm, W_C), lambda n, i: (n, i, 0)),
                  pl.BlockSpec((tm, MLA_HEAD_PAD), lambda n, i: (i, 0)),
                  pl.BlockSpec((tm, MLA_HEAD_PAD), lambda n, i: (i, 0)),
                  const((1, C_Q_RANK)), const((1, C_KV_RANK)),
                  const((C_Q_RANK, 2 * hw)), const((C_KV_RANK, hw)), const((GROUP_W, C_KV_RANK))],
        out_specs=[pl.BlockSpec((1, tm, hw), lambda n, i: (n, i, 0)),
                   pl.BlockSpec((1, tm, hw), lambda n, i: (n, i, 0)),
                   pl.BlockSpec((1, C_HEADS * MLA_V_ROWS, tm), lambda n, i: (n, 0, i))],
        out_shape=[jax.ShapeDtypeStruct((b, s, hw), BF16),
                   jax.ShapeDtypeStruct((b, s, hw), BF16),
                   jax.ShapeDtypeStruct((b, C_HEADS * MLA_V_ROWS, s), BF16)],
        compiler_params=_cparams(2),
        name="mla_prep",
    )(c3, ct, st, qn, kvn, wq, wk, wvt)


def _mla_flash_kernel(q_ref, k_ref, vt_ref, o_ref, m_sc, acc_sc, s_sc, *, kc, depth):
    hp, hv = MLA_HEAD_PAD, MLA_V_ROWS
    kv = pl.program_id(2)
    tk = k_ref.shape[1]

    @pl.when(kv == 0)
    def _():
        m_sc[...] = jnp.full(m_sc.shape, NEG_INF, F32)
        acc_sc[...] = jnp.zeros(acc_sc.shape, F32)

    units = [(c, h) for c in range(tk // kc) for h in range(C_HEADS)]

    def scores(u):
        c, h = units[u]
        s_sc[u % depth] = lax.dot_general(k_ref[0, c * kc:(c + 1) * kc, h * hp:(h + 1) * hp],
                                          q_ref[0, :, h * hp:(h + 1) * hp],
                                          (((1,), (1,)), ((), ())), preferred_element_type=F32)

    for u in range(min(depth, len(units))):
        scores(u)
    for u, (c, h) in enumerate(units):
        st = s_sc[u % depth]
        m_old = m_sc[h:h + 1, :]
        m_new = jnp.maximum(m_old, jnp.max(st, axis=0, keepdims=True))
        alpha = jnp.exp2(m_old - m_new)
        pt = jnp.exp2(st - m_new).astype(BF16)
        m_sc[h:h + 1, :] = m_new
        if u + depth < len(units):
            scores(u + depth)
        rows = slice(h * hv, (h + 1) * hv)
        acc_sc[rows, :] = alpha * acc_sc[rows, :] + jnp.dot(
            vt_ref[0, rows, c * kc:(c + 1) * kc], pt, preferred_element_type=F32)

    @pl.when(kv == pl.num_programs(2) - 1)
    def _():
        ot = jnp.concatenate(
            [acc_sc[h * hv:h * hv + HEAD_DIM, :] / acc_sc[h * hv + HEAD_DIM:h * hv + HEAD_DIM + 1, :]
             for h in range(C_HEADS)], axis=0)
        o_ref[0] = ot.T.astype(o_ref.dtype)


def _mla_flash(q, k, vt, tq=512, tk=4096, kc=256, depth=8):
    b, s, hw = q.shape
    tq = min(tq, s)
    tk = min(tk, s)
    vrows = C_HEADS * MLA_V_ROWS
    return pl.pallas_call(
        functools.partial(_mla_flash_kernel, kc=kc, depth=depth),
        grid=(b, s // tq, s // tk),
        in_specs=[pl.BlockSpec((1, tq, hw), lambda n, i, j: (n, i, 0)),
                  pl.BlockSpec((1, tk, hw), lambda n, i, j: (n, j, 0)),
                  pl.BlockSpec((1, vrows, tk), lambda n, i, j: (n, 0, j))],
        out_specs=pl.BlockSpec((1, tq, GROUP_W), lambda n, i, j: (n, i, 0)),
        out_shape=jax.ShapeDtypeStruct((b, s, GROUP_W), BF16),
        scratch_shapes=[pltpu.VMEM((8, tq), F32), pltpu.VMEM((vrows, tq), F32),
                        pltpu.VMEM((depth, kc, tq), F32)],
        compiler_params=_cparams(3),
        name="mla_flash",
    )(q, k, vt)


def _rope_tables(seq):
    inv_freq = 1.0 / (ROPE_THETA ** (jnp.arange(0, C_ROPE, 2, dtype=F32) / C_ROPE))
    ang = jnp.arange(seq, dtype=F32)[:, None] * inv_freq[None, :]
    cos, sin = jnp.cos(ang), jnp.sin(ang)
    pad = MLA_HEAD_PAD - C_NOPE - C_ROPE
    ct = jnp.concatenate([jnp.ones((seq, C_NOPE), F32), cos, cos, jnp.zeros((seq, pad), F32)], axis=1)
    st = jnp.concatenate([jnp.zeros((seq, C_NOPE), F32), -sin, sin, jnp.zeros((seq, pad), F32)], axis=1)
    return ct, st


def _outproj_kernel(a1_ref, a2_ref, a3_ref, l1_ref, l2_ref, l3_ref, ob_ref, oc_ref, od_ref,
                    x_ref, g_ref, w_ref, o_ref):
    ls = [l1_ref[...], l2_ref[...], l3_ref[...]]
    m = jnp.maximum(jnp.maximum(ls[0], ls[1]), ls[2])
    es = [jnp.exp(l - m) for l in ls]
    den = es[0] + es[1] + es[2]
    oa = sum((e / den) * a_ref[...].astype(F32) for e, a_ref in zip(es, (a1_ref, a2_ref, a3_ref)))
    groups = [oa, ob_ref[...].astype(F32), oc_ref[...].astype(F32), od_ref[...].astype(F32)]
    y = jnp.concatenate([_rms(o, g_ref[i:i + 1, :]).astype(BF16) for i, o in enumerate(groups)], axis=1)
    o_ref[...] = x_ref[...] + jnp.dot(y, w_ref[...], preferred_element_type=F32)


def _outproj(a_outs, ob, oc, od, x2, g, w, tm=512):
    t = x2.shape[0]
    tok = lambda n: pl.BlockSpec((tm, n), lambda i: (i, 0))
    args = [o for o, _ in a_outs] + [l for _, l in a_outs] + [ob, oc, od, x2, g, w]
    return pl.pallas_call(
        _outproj_kernel,
        grid=(t // tm,),
        in_specs=[tok(GROUP_W)] * 9 + [tok(D_MODEL),
                                       pl.BlockSpec((N_GROUPS, GROUP_W), lambda i: (0, 0)),
                                       pl.BlockSpec((D_MODEL, D_MODEL), lambda i: (0, 0))],
        out_specs=tok(D_MODEL),
        out_shape=jax.ShapeDtypeStruct((t, D_MODEL), F32),
        compiler_params=_cparams(1),
        name="outproj",
    )(*args)


def _mlp_kernel(x_ref, g_ref, wu_ref, wd_ref, gf_ref, o_ref, *, final, ff_chunk):
    x = x_ref[...]
    h = _rms(x, g_ref[...]).astype(BF16)
    acc = x
    for c in range(D_FF // ff_chunk):
        cols = slice(c * ff_chunk, (c + 1) * ff_chunk)
        u = jnp.dot(h, wu_ref[:, cols], preferred_element_type=F32)
        u = jnp.square(jnp.maximum(u, 0.0)).astype(BF16)
        acc = acc + jnp.dot(u, wd_ref[cols, :], preferred_element_type=F32)
    if final:
        acc = _rms(acc, gf_ref[...])
    o_ref[...] = acc


def _mlp(x2, g, wu, wd, gf, final, tm=512, ff_chunk=1024):
    t = x2.shape[0]
    return pl.pallas_call(
        functools.partial(_mlp_kernel, final=final, ff_chunk=ff_chunk),
        grid=(t // tm,),
        in_specs=[pl.BlockSpec((tm, D_MODEL), lambda i: (i, 0)),
                  pl.BlockSpec((1, D_MODEL), lambda i: (0, 0)),
                  pl.BlockSpec((D_MODEL, D_FF), lambda i: (0, 0)),
                  pl.BlockSpec((D_FF, D_MODEL), lambda i: (0, 0)),
                  pl.BlockSpec((1, D_MODEL), lambda i: (0, 0))],
        out_specs=pl.BlockSpec((tm, D_MODEL), lambda i: (i, 0)),
        out_shape=jax.ShapeDtypeStruct((t, D_MODEL), F32),
        compiler_params=_cparams(1),
        name="mlp_final" if final else "mlp",
    )(x2, g, wu, wd, gf)


def _gqa_perm():
    return np.concatenate([np.arange(h * HEAD_DIM, (h + 1) * HEAD_DIM) for h in GQA_HEAD_ORDER])


def _layer_weights(lp):
    (norm_attn, w_in, mla_q_norm, w_q_up, mla_kv_norm, w_kv_up, sink_logits, na_rpb,
     group_norm, w_out, norm_mlp, w_mlp_up, w_mlp_down) = lp
    perm = _gqa_perm()
    half = C_ROPE // 2
    z = lambda n: jnp.zeros((D_MODEL, n), w_in.dtype)
    kr = w_in[:, 1664:1696]
    kr_sw = jnp.concatenate([kr[:, half:], kr[:, :half]], axis=1)
    pad = MLA_HEAD_PAD - C_NOPE - C_ROPE
    qscale = HEAD_DIM ** -0.5 * LOG2E
    w_in2 = jnp.concatenate([
        w_in[:, 0:256] * qscale, w_in[:, 256:768],
        w_in[:, 768:1024][:, perm] * qscale, w_in[:, 1024:1152],
        w_in[:, 1696:1952] * qscale, w_in[:, 1952:2208],
        w_in[:, 1280:1664],
        z(C_NOPE), kr, z(pad), z(C_NOPE), kr_sw, z(pad),
    ], axis=1).astype(BF16)

    wq = w_q_up.reshape(C_Q_RANK, C_HEADS, C_NOPE + C_ROPE)
    zq = lambda n: jnp.zeros((C_Q_RANK, C_HEADS, n), w_q_up.dtype)
    rope = wq[:, :, C_NOPE:]
    rope_sw = jnp.concatenate([rope[:, :, half:], rope[:, :, :half]], axis=2)
    wq_main = jnp.concatenate([wq, zq(pad)], axis=2).reshape(C_Q_RANK, -1)
    wq_swap = jnp.concatenate([zq(C_NOPE), rope_sw, zq(pad)], axis=2).reshape(C_Q_RANK, -1)
    wq2 = jnp.concatenate([wq_main, wq_swap], axis=1).astype(BF16)

    wkv = w_kv_up.reshape(C_KV_RANK, C_HEADS, C_NOPE + HEAD_DIM)
    wk2 = jnp.concatenate([wkv[:, :, :C_NOPE], jnp.zeros((C_KV_RANK, C_HEADS, MLA_HEAD_PAD - C_NOPE), w_kv_up.dtype)],
                          axis=2).reshape(C_KV_RANK, -1).astype(BF16)
    wvt = wkv[:, :, C_NOPE:].reshape(C_KV_RANK, -1).T.astype(BF16)

    wvt_bd = jnp.concatenate([w_in[:, 1152:1280], w_in[:, 2208:2464]], axis=1).T.astype(BF16)
    return dict(
        norm_attn=norm_attn.reshape(1, -1), w_in=w_in2, wvt_bd=wvt_bd,
        qn=mla_q_norm.reshape(1, -1), kvn=mla_kv_norm.reshape(1, -1), wq=wq2, wk=wk2, wvt=wvt,
        sink=sink_logits[jnp.asarray(GQA_HEAD_ORDER)], rpb=na_rpb,
        gn=group_norm, w_out=w_out.astype(BF16), norm_mlp=norm_mlp.reshape(1, -1),
        w_up=w_mlp_up.astype(BF16), w_down=w_mlp_down.astype(BF16))


def _layer(x2, bsz, seq, lw, tables, gf, final):
    ct, st = tables
    qkv_a, qk_b, qk_d, c, vt_b, vt_d = _inproj(x2, lw["norm_attn"], lw["w_in"], lw["wvt_bd"])
    a_outs = _mixer_a(qkv_a, bsz, seq)
    ob = _mixer_b(qk_b, vt_b, lw["sink"], bsz, seq)
    q, k, vt = _mla_prep(c.reshape(bsz, seq, W_C), ct, st, lw["qn"], lw["kvn"], lw["wq"], lw["wk"], lw["wvt"])
    oc = _mla_flash(q, k, vt).reshape(bsz * seq, GROUP_W)
    od = _mixer_d(qk_d, vt_d, lw["rpb"], bsz, seq)
    x2 = _outproj(a_outs, ob, oc, od, x2, lw["gn"], lw["w_out"])
    return _mlp(x2, lw["norm_mlp"], lw["w_up"], lw["w_down"], gf, final)


def _trunk(x, layer_ws, gf):
    bsz, seq, _ = x.shape
    tables = _rope_tables(seq)
    x2 = x.reshape(bsz * seq, D_MODEL)
    for i, lw in enumerate(layer_ws):
        x2 = _layer(x2, bsz, seq, lw, tables, gf, final=(i == len(layer_ws) - 1))
    return x2.reshape(bsz, seq, D_MODEL)


def kernel(x_prompt, x_sample, norm_attn, w_in, mla_q_norm, w_q_up, mla_kv_norm, w_kv_up, sink_logits, na_rpb, group_norm, w_out, norm_mlp, w_mlp_up, w_mlp_down, norm_final):
    params = (norm_attn, w_in, mla_q_norm, w_q_up, mla_kv_norm, w_kv_up, sink_logits, na_rpb,
              group_norm, w_out, norm_mlp, w_mlp_up, w_mlp_down)
    depth = norm_attn.shape[0]
    layer_ws = [_layer_weights(tuple(p[i] for p in params)) for i in range(depth)]
    gf = norm_final.reshape(1, -1)
    return (_trunk(x_prompt, layer_ws, gf), _trunk(x_sample, layer_ws, gf))
```

```python
import functools

import numpy as np
import jax
import jax.numpy as jnp
from jax import lax
from jax.experimental import pallas as pl
from jax.experimental.pallas import tpu as pltpu

F32 = jnp.float32
BF16 = jnp.bfloat16

D_MODEL = 1024
HEAD_DIM = 64
N_GROUPS = 4
GROUP_W = 256
A_CONFIGS = ((128, 1), (512, 4), (2048, 16))
B_HALF_WINDOW = 128
C_Q_RANK = 256
C_KV_RANK = 128
C_NOPE = 64
C_ROPE = 32
C_HEADS = 4
ROPE_THETA = 10000.0
GRID_W = 64
NA_KH = 8
NA_KW = 16
D_FF = 4096
EPS = 1e-5
NEG_INF = -1e30

W_A = 768
W_B = 384
W_D = 512
W_VT = 384
W_C = 640
MLA_HEAD_PAD = 128
MLA_V_ROWS = 80
LOG2E = 1.4426950408889634
LANES = 128

VMEM_LIMIT = 56 * 1024 * 1024


def _cparams(n_axes):
    return pltpu.CompilerParams(dimension_semantics=("arbitrary",) * n_axes,
                                vmem_limit_bytes=VMEM_LIMIT)


def _rms(x, g):
    return x * lax.rsqrt(jnp.mean(x * x, axis=-1, keepdims=True) + EPS) * g


def _inproj_kernel(x_ref, g_ref, w_ref, wvt_ref, oa_ref, oa4_ref, oa16_ref, ob_ref, od_ref, oc_ref,
                   vtb_ref, vtd_ref, a_sc):
    h = _rms(x_ref[...], g_ref[...]).astype(BF16)
    tm = h.shape[0]
    qkv_a = jnp.dot(h, w_ref[:, :W_A], preferred_element_type=F32)
    oa_ref[...] = qkv_a.astype(oa_ref.dtype)
    for c in range(W_A // LANES):
        a_sc[c] = qkv_a[:, c * LANES:(c + 1) * LANES]
    for dil, o_ref in ((4, oa4_ref), (16, oa16_ref)):
        for r in range(dil):
            o_ref[0, r] = jnp.concatenate(
                [a_sc[c, pl.ds(r, tm // dil, stride=dil), :] for c in range(W_A // LANES)],
                axis=1).astype(o_ref.dtype)
    off = W_A
    for o_ref in (ob_ref, od_ref, oc_ref):
        n = o_ref.shape[-1]
        o_ref[...] = jnp.dot(h, w_ref[:, off:off + n], preferred_element_type=F32).astype(o_ref.dtype)
        off += n
    vt = lax.dot_general(wvt_ref[...], h, (((1,), (1,)), ((), ())), preferred_element_type=F32)
    nb = vtb_ref.shape[0]
    vtb_ref[...] = vt[:nb].astype(vtb_ref.dtype)
    vtd_ref[...] = vt[nb:].astype(vtd_ref.dtype)


def _inproj(x2, g, w, wvt, bsz, seq, tm=512):
    t = x2.shape[0]
    nt = seq // tm
    widths = (W_B, W_D, W_C)
    vrows = (W_VT - GROUP_W, GROUP_W)
    tok = lambda n: pl.BlockSpec((tm, n), lambda b, i: (b * nt + i, 0))
    plane = lambda dil: pl.BlockSpec((1, dil, tm // dil, W_A), lambda b, i: (b, 0, i, 0))
    return pl.pallas_call(
        _inproj_kernel,
        grid=(bsz, nt),
        in_specs=[tok(D_MODEL),
                  pl.BlockSpec((1, D_MODEL), lambda b, i: (0, 0)),
                  pl.BlockSpec((D_MODEL, W_A + sum(widths)), lambda b, i: (0, 0)),
                  pl.BlockSpec((W_VT, D_MODEL), lambda b, i: (0, 0))],
        out_specs=([tok(W_A), plane(4), plane(16)] + [tok(n) for n in widths]
                   + [pl.BlockSpec((r, tm), lambda b, i: (0, b * nt + i)) for r in vrows]),
        out_shape=([jax.ShapeDtypeStruct((t, W_A), BF16)]
                   + [jax.ShapeDtypeStruct((bsz, dil, seq // dil, W_A), BF16) for dil in (4, 16)]
                   + [jax.ShapeDtypeStruct((t, n), BF16) for n in widths]
                   + [jax.ShapeDtypeStruct((r, t), BF16) for r in vrows]),
        scratch_shapes=[pltpu.VMEM((W_A // LANES, tm, LANES), F32)],
        compiler_params=_cparams(2),
        name="inproj",
    )(x2, g, w, wvt)


def _win_attn_t_kernel(*refs, heads, ck, tq, hb, n_tiles, has_sink, has_lse, v_rows):
    q_ref, kp_ref, km_ref, kn_ref, vp_ref, vm_ref, vn_ref, bias_ref = refs[:8]
    sink_ref = refs[8] if has_sink else None
    o_ref = refs[-2] if has_lse else refs[-1]
    lse_ref = refs[-1] if has_lse else None
    ts = q_ref.shape[0]
    tkw = tq + 2 * hb
    nh = len(heads)
    kw = jnp.concatenate([kp_ref[...], km_ref[...], kn_ref[...]], axis=0)
    if v_rows:
        vw = jnp.concatenate([vp_ref[...], vm_ref[...], vn_ref[...]], axis=0).astype(F32).T.astype(BF16)
    else:
        vw = jnp.concatenate([vp_ref[...], vm_ref[...], vn_ref[...]], axis=1)
    lane = lax.broadcasted_iota(jnp.int32, (tq, ck), 1)
    masks = [(lane >= lo) & (lane < lo + HEAD_DIM) for (_, lo, _, _) in heads]
    ones = jnp.ones((MLA_V_ROWS - HEAD_DIM, tkw), vw.dtype)
    step = pl.program_id(0)
    for j in range(ts // tq):
        tile = (step * (ts // tq) + j) % n_tiles
        kind = jnp.where(tile == 0, 0, jnp.where(tile == n_tiles - 1, 2, 1))
        q = q_ref[j * tq:(j + 1) * tq, :]
        q_stack = jnp.concatenate(
            [jnp.where(mk, q[:, g * ck:(g + 1) * ck], jnp.zeros((), q.dtype))
             for (g, _, _, _), mk in zip(heads, masks)], axis=0)
        st = lax.dot_general(kw[j * tq:j * tq + tkw], q_stack, (((1,), (1,)), ((), ())),
                             preferred_element_type=F32) + bias_ref[kind]
        m = jnp.max(st, axis=0, keepdims=True)
        if has_sink:
            sink = jnp.concatenate([jnp.full((1, tq), sink_ref[i] * LOG2E, F32) for i in range(nh)], axis=1)
            m = jnp.maximum(m, sink)
            sink_p = jnp.exp2(sink - m)
        pt = jnp.exp2(st - m).astype(BF16)
        outs = [None] * nh
        lses = [None] * nh
        for i, (_, _, vrow, out_pos) in enumerate(heads):
            lhs = jnp.concatenate([vw[vrow:vrow + HEAD_DIM, j * tq:j * tq + tkw], ones], axis=0)
            r = jnp.dot(lhs, pt[:, i * tq:(i + 1) * tq], preferred_element_type=F32)
            den = r[HEAD_DIM:HEAD_DIM + 1]
            if has_sink:
                den = den + sink_p[:, i * tq:(i + 1) * tq]
            outs[out_pos] = r[:HEAD_DIM] / den
            if has_lse:
                lse = (m[:, i * tq:(i + 1) * tq] + jnp.log2(den)) * (1.0 / LOG2E)
                lses[out_pos] = jnp.broadcast_to(lse, (HEAD_DIM, tq))
        o_ref[j * tq:(j + 1) * tq, :] = jnp.concatenate(outs, axis=0).T.astype(o_ref.dtype)
        if has_lse:
            lse_ref[j * tq:(j + 1) * tq, :] = jnp.concatenate(lses, axis=0).T


def _win_attn_t(qk, vt, bias, sink, *, seq, ts, tq, hb, cq, ck, qcol, kcol, vcol, heads, has_lse, name):
    rows = qk.shape[0]
    ts = min(ts, rows)
    tq = min(tq, seq)
    per = ts // hb
    nhb = rows // hb
    nh = len(heads)
    tkw = tq + 2 * hb
    v_rows = vt is None

    def prev_idx(i):
        return jnp.maximum(i * per - 1, 0)

    def next_idx(i):
        return jnp.minimum((i + 1) * per, nhb - 1)

    in_specs = [
        pl.BlockSpec((ts, cq), lambda i: (i, qcol)),
        pl.BlockSpec((hb, ck), lambda i: (prev_idx(i), kcol)),
        pl.BlockSpec((ts, ck), lambda i: (i, kcol)),
        pl.BlockSpec((hb, ck), lambda i: (next_idx(i), kcol)),
    ]
    if v_rows:
        in_specs += [
            pl.BlockSpec((hb, GROUP_W), lambda i: (prev_idx(i), vcol)),
            pl.BlockSpec((ts, GROUP_W), lambda i: (i, vcol)),
            pl.BlockSpec((hb, GROUP_W), lambda i: (next_idx(i), vcol)),
        ]
        args = [qk] * 7
    else:
        cv = vt.shape[0]
        in_specs += [
            pl.BlockSpec((cv, hb), lambda i: (0, prev_idx(i))),
            pl.BlockSpec((cv, ts), lambda i: (0, i)),
            pl.BlockSpec((cv, hb), lambda i: (0, next_idx(i))),
        ]
        args = [qk] * 4 + [vt] * 3
    in_specs.append(pl.BlockSpec((3, tkw, nh * tq), lambda i: (0, 0, 0)))
    args.append(bias)
    if sink is not None:
        in_specs.append(pl.BlockSpec(memory_space=pltpu.SMEM))
        args.append(sink)
    out_specs = [pl.BlockSpec((ts, GROUP_W), lambda i: (i, 0))]
    out_shape = [jax.ShapeDtypeStruct((rows, GROUP_W), BF16)]
    if has_lse:
        out_specs.append(pl.BlockSpec((ts, GROUP_W), lambda i: (i, 0)))
        out_shape.append(jax.ShapeDtypeStruct((rows, GROUP_W), F32))
    return pl.pallas_call(
        functools.partial(_win_attn_t_kernel, heads=heads, ck=ck, tq=tq, hb=hb, n_tiles=seq // tq,
                          has_sink=sink is not None, has_lse=has_lse, v_rows=v_rows),
        grid=(rows // ts,),
        in_specs=in_specs,
        out_specs=out_specs,
        out_shape=out_shape,
        compiler_params=_cparams(1),
        name=name,
    )(*args)


def _transpose_bias(bias):
    return jnp.swapaxes(bias, 1, 2) * LOG2E


def _tile_indices(nt):
    return (0, min(1, nt - 1), nt - 1)


def _banded_bias(seq, tq, hb, hw, slopes, dist_scale):
    nt = seq // tq
    tabs = []
    for ti in _tile_indices(nt):
        qpos = ti * tq + np.arange(tq)[:, None]
        kpos = ti * tq - hb + np.arange(tq + 2 * hb)[None, :]
        dist = np.abs(kpos - qpos)
        valid = (dist <= hw) & (kpos >= 0) & (kpos < seq)
        pen = -np.asarray(slopes, np.float32)[:, None, None] * (dist.astype(np.float32) * np.float32(dist_scale))[None]
        tabs.append(np.where(valid[None], pen, np.float32(NEG_INF)).reshape(-1, tq + 2 * hb))
    return jnp.asarray(np.stack(tabs).astype(np.float32))


def _na_bias(rpb, seq, rows_per_tile):
    rows = seq // GRID_W
    kh = min(NA_KH, rows)
    nt = rows // rows_per_tile
    tq = rows_per_tile * GRID_W
    cq = np.arange(GRID_W)[:, None]
    ck = np.arange(GRID_W)[None, :]
    col_start = np.clip(cq - NA_KW // 2, 0, GRID_W - NA_KW)
    col_ok = (ck >= col_start) & (ck < col_start + NA_KW)
    dc = np.clip(ck - cq, -(NA_KW - 1), NA_KW - 1) + (NA_KW - 1)
    col_hot = jnp.asarray(np.eye(2 * NA_KW - 1, dtype=np.float32)[dc])
    tabs = []
    for ti in _tile_indices(nt):
        rq = (ti * rows_per_tile + np.arange(rows_per_tile))[:, None]
        rk = ((ti - 1) * rows_per_tile + np.arange(3 * rows_per_tile))[None, :]
        row_start = np.clip(rq - kh // 2, 0, rows - kh)
        row_ok = (rk >= row_start) & (rk < row_start + kh)
        dr = np.clip(rk - rq + (NA_KH - 1), 0, 2 * NA_KH - 2)
        row_hot = jnp.asarray(np.eye(2 * NA_KH - 1, dtype=np.float32)[dr])
        vals = jnp.einsum("abr,hrc,xyc->haxby", row_hot, rpb.astype(F32), col_hot,
                          precision=lax.Precision.HIGHEST)
        valid = row_ok[:, None, :, None] & col_ok[None, :, None, :]
        tabs.append(jnp.where(valid[None], vals, NEG_INF).reshape(-1, 3 * tq))
    return jnp.stack(tabs)


def _alibi_slopes():
    s = np.exp2(-8.0 * np.arange(1, 9, dtype=np.float32) / 8.0)
    return s[1::2], s[0::2]


GQA_HEAD_ORDER = (0, 2, 1, 3)
HEADS_T_FULL = tuple((0, 64 * h, 64 * h, h) for h in range(4))
HEADS_T_GQA = ((0, 0, 0, 0), (0, 64, 64, 2), (1, 0, 0, 1), (1, 64, 64, 3))

A_HALO = 128


def _mixer_a(qkv_planes, seq):
    slopes_a, _ = _alibi_slopes()
    outs = []
    for (window, dil), qkv in zip(A_CONFIGS, qkv_planes):
        length = seq // dil
        hw = window // (2 * dil)
        tq = min(256, length)
        bias = _banded_bias(length, tq, A_HALO, hw, slopes_a, float(dil))
        outs.append(_win_attn_t(
            qkv.reshape(-1, W_A), None, _transpose_bias(bias), None, seq=length, ts=1024, tq=tq, hb=A_HALO,
            cq=GROUP_W, ck=GROUP_W, qcol=0, kcol=1, vcol=2, heads=HEADS_T_FULL, has_lse=True,
            name=f"mix_a_d{dil}"))
    return outs


def _mixer_b(qk_b, vt_b, sink, seq):
    _, slopes_b = _alibi_slopes()
    tq = 256
    bias = _banded_bias(seq, tq, B_HALF_WINDOW, B_HALF_WINDOW, slopes_b[list(GQA_HEAD_ORDER)], 1.0)
    (o,) = _win_attn_t(
        qk_b, vt_b, _transpose_bias(bias), sink, seq=seq, ts=1024, tq=tq, hb=B_HALF_WINDOW,
        cq=GROUP_W, ck=128, qcol=0, kcol=2, vcol=None, heads=HEADS_T_GQA, has_lse=False, name="mix_b")
    return o


def _mixer_d(qk_d, vt_d, rpb, seq):
    rows_per_tile = 4
    tq = rows_per_tile * GRID_W
    bias = _na_bias(rpb, seq, rows_per_tile)
    (o,) = _win_attn_t(
        qk_d, vt_d, _transpose_bias(bias), None, seq=seq, ts=1024, tq=tq, hb=tq,
        cq=GROUP_W, ck=GROUP_W, qcol=0, kcol=1, vcol=None, heads=HEADS_T_FULL, has_lse=False, name="mix_d")
    return o


def _mla_prep_kernel(c_ref, ct_ref, st_ref, qn_ref, kvn_ref, wq_ref, wk_ref, wvt_ref,
                     q_ref, k_ref, vt_ref):
    hp = MLA_HEAD_PAD
    c = c_ref[0]
    ct = ct_ref[...]
    st = st_ref[...]
    cqn = _rms(c[:, :C_Q_RANK].astype(F32), qn_ref[...]).astype(BF16)
    qq = jnp.dot(cqn, wq_ref[...], preferred_element_type=F32)
    scale = (C_NOPE + C_ROPE) ** -0.5 * LOG2E
    q_ref[0] = jnp.concatenate(
        [(qq[:, h * hp:(h + 1) * hp] * ct + qq[:, (C_HEADS + h) * hp:(C_HEADS + h + 1) * hp] * st) * scale
         for h in range(C_HEADS)], axis=1).astype(q_ref.dtype)
    ckvn = _rms(c[:, C_Q_RANK:C_Q_RANK + C_KV_RANK].astype(F32), kvn_ref[...]).astype(BF16)
    kk = jnp.dot(ckvn, wk_ref[...], preferred_element_type=F32)
    o0 = C_Q_RANK + C_KV_RANK
    kpe = c[:, o0:o0 + hp].astype(F32) * ct + c[:, o0 + hp:o0 + 2 * hp].astype(F32) * st
    k_ref[0] = jnp.concatenate([kk[:, h * hp:(h + 1) * hp] + kpe for h in range(C_HEADS)],
                               axis=1).astype(k_ref.dtype)
    vt = lax.dot_general(wvt_ref[...], ckvn, (((1,), (1,)), ((), ())), preferred_element_type=F32)
    ones = jnp.ones((MLA_V_ROWS - HEAD_DIM, vt.shape[1]), F32)
    vt_ref[0] = jnp.concatenate(
        [blk for h in range(C_HEADS) for blk in (vt[h * HEAD_DIM:(h + 1) * HEAD_DIM], ones)],
        axis=0).astype(vt_ref.dtype)


def _mla_prep(c3, ct, st, qn, kvn, wq, wk, wvt, tm=512):
    b, s, _ = c3.shape
    hw = C_HEADS * MLA_HEAD_PAD
    const = lambda shape: pl.BlockSpec(shape, lambda n, i: (0,) * len(shape))
    return pl.pallas_call(
        _mla_prep_kernel,
        grid=(b, s // tm),
        in_specs=[pl.BlockSpec((1, tm, W_C), lambda n, i: (n, i, 0)),
                  pl.BlockSpec((tm, MLA_HEAD_PAD), lambda n, i: (i, 0)),
                  pl.BlockSpec((tm, MLA_HEAD_PAD), lambda n, i: (i, 0)),
                  const((1, C_Q_RANK)), const((1, C_KV_RANK)),
                  const((C_Q_RANK, 2 * hw)), const((C_KV_RANK, hw)), const((GROUP_W, C_KV_RANK))],
        out_specs=[pl.BlockSpec((1, tm, hw), lambda n, i: (n, i, 0)),
                   pl.BlockSpec((1, tm, hw), lambda n, i: (n, i, 0)),
                   pl.BlockSpec((1, C_HEADS * MLA_V_ROWS, tm), lambda n, i: (n, 0, i))],
        out_shape=[jax.ShapeDtypeStruct((b, s, hw), BF16),
                   jax.ShapeDtypeStruct((b, s, hw), BF16),
                   jax.ShapeDtypeStruct((b, C_HEADS * MLA_V_ROWS, s), BF16)],
        compiler_params=_cparams(2),
        name="mla_prep",
    )(c3, ct, st, qn, kvn, wq, wk, wvt)


def _mla_flash_kernel(q_ref, k_ref, vt_ref, o_ref, m_sc, acc_sc, s_sc, *, kc, depth):
    hp, hv = MLA_HEAD_PAD, MLA_V_ROWS
    kv = pl.program_id(2)
    tk = k_ref.shape[1]

    @pl.when(kv == 0)
    def _():
        m_sc[...] = jnp.full(m_sc.shape, NEG_INF, F32)
        acc_sc[...] = jnp.zeros(acc_sc.shape, F32)

    units = [(c, h) for c in range(tk // kc) for h in range(C_HEADS)]

    def scores(u):
        c, h = units[u]
        s_sc[u % depth] = lax.dot_general(k_ref[0, c * kc:(c + 1) * kc, h * hp:(h + 1) * hp],
                                          q_ref[0, :, h * hp:(h + 1) * hp],
                                          (((1,), (1,)), ((), ())), preferred_element_type=F32)

    for u in range(min(depth, len(units))):
        scores(u)
    for u, (c, h) in enumerate(units):
        st = s_sc[u % depth]
        m_old = m_sc[h:h + 1, :]
        m_new = jnp.maximum(m_old, jnp.max(st, axis=0, keepdims=True))
        alpha = jnp.exp2(m_old - m_new)
        pt = jnp.exp2(st - m_new).astype(BF16)
        m_sc[h:h + 1, :] = m_new
        if u + depth < len(units):
            scores(u + depth)
        rows = slice(h * hv, (h + 1) * hv)
        acc_sc[rows, :] = alpha * acc_sc[rows, :] + jnp.dot(
            vt_ref[0, rows, c * kc:(c + 1) * kc], pt, preferred_element_type=F32)

    @pl.when(kv == pl.num_programs(2) - 1)
    def _():
        ot = jnp.concatenate(
            [acc_sc[h * hv:h * hv + HEAD_DIM, :] / acc_sc[h * hv + HEAD_DIM:h * hv + HEAD_DIM + 1, :]
             for h in range(C_HEADS)], axis=0)
        o_ref[0] = ot.T.astype(o_ref.dtype)


def _mla_flash(q, k, vt, tq=512, tk=4096, kc=256, depth=8):
    b, s, hw = q.shape
    tq = min(tq, s)
    tk = min(tk, s)
    vrows = C_HEADS * MLA_V_ROWS
    return pl.pallas_call(
        functools.partial(_mla_flash_kernel, kc=kc, depth=depth),
        grid=(b, s // tq, s // tk),
        in_specs=[pl.BlockSpec((1, tq, hw), lambda n, i, j: (n, i, 0)),
                  pl.BlockSpec((1, tk, hw), lambda n, i, j: (n, j, 0)),
                  pl.BlockSpec((1, vrows, tk), lambda n, i, j: (n, 0, j))],
        out_specs=pl.BlockSpec((1, tq, GROUP_W), lambda n, i, j: (n, i, 0)),
        out_shape=jax.ShapeDtypeStruct((b, s, GROUP_W), BF16),
        scratch_shapes=[pltpu.VMEM((8, tq), F32), pltpu.VMEM((vrows, tq), F32),
                        pltpu.VMEM((depth, kc, tq), F32)],
        compiler_params=_cparams(3),
        name="mla_flash",
    )(q, k, vt)


def _rope_tables(seq):
    inv_freq = 1.0 / (ROPE_THETA ** (jnp.arange(0, C_ROPE, 2, dtype=F32) / C_ROPE))
    ang = jnp.arange(seq, dtype=F32)[:, None] * inv_freq[None, :]
    cos, sin = jnp.cos(ang), jnp.sin(ang)
    pad = MLA_HEAD_PAD - C_NOPE - C_ROPE
    ct = jnp.concatenate([jnp.ones((seq, C_NOPE), F32), cos, cos, jnp.zeros((seq, pad), F32)], axis=1)
    st = jnp.concatenate([jnp.zeros((seq, C_NOPE), F32), -sin, sin, jnp.zeros((seq, pad), F32)], axis=1)
    return ct, st


def _outproj_kernel(a1_ref, a2_ref, a3_ref, l1_ref, l2_ref, l3_ref, ob_ref, oc_ref, od_ref,
                    x_ref, g_ref, w_ref, o_ref, *scratch):
    tm = x_ref.shape[0]
    ncol = GROUP_W // LANES

    def natural(ref, dil, sc):
        for r in range(dil):
            for c in range(ncol):
                sc[c, pl.ds(r, tm // dil, stride=dil), :] = ref[0, r, :, c * LANES:(c + 1) * LANES].astype(F32)
        return jnp.concatenate([sc[c] for c in range(ncol)], axis=1)

    a_s = [a1_ref[...].astype(F32), natural(a2_ref, 4, scratch[0]), natural(a3_ref, 16, scratch[1])]
    ls = [l1_ref[...], natural(l2_ref, 4, scratch[2]), natural(l3_ref, 16, scratch[3])]
    m = jnp.maximum(jnp.maximum(ls[0], ls[1]), ls[2])
    es = [jnp.exp(l - m) for l in ls]
    den = es[0] + es[1] + es[2]
    oa = sum((e / den) * a for e, a in zip(es, a_s))
    groups = [oa, ob_ref[...].astype(F32), oc_ref[...].astype(F32), od_ref[...].astype(F32)]
    y = jnp.concatenate([_rms(o, g_ref[i:i + 1, :]).astype(BF16) for i, o in enumerate(groups)], axis=1)
    o_ref[...] = x_ref[...] + jnp.dot(y, w_ref[...], preferred_element_type=F32)


def _outproj(a_outs, ob, oc, od, x2, g, w, bsz, seq, tm=512):
    t = x2.shape[0]
    nt = seq // tm
    tok = lambda n: pl.BlockSpec((tm, n), lambda b, i: (b * nt + i, 0))
    plane = lambda dil: pl.BlockSpec((1, dil, tm // dil, GROUP_W), lambda b, i: (b, 0, i, 0))
    dils = [dil for _, dil in A_CONFIGS]
    a_specs = [tok(GROUP_W) if dil == 1 else plane(dil) for dil in dils]
    planes = lambda arr, dil: arr if dil == 1 else arr.reshape(bsz, dil, seq // dil, GROUP_W)
    args = ([planes(o, dil) for (o, _), dil in zip(a_outs, dils)] + [planes(l, dil) for (_, l), dil in zip(a_outs, dils)]
            + [ob, oc, od, x2, g, w])
    return pl.pallas_call(
        _outproj_kernel,
        grid=(bsz, nt),
        in_specs=a_specs + a_specs + [tok(GROUP_W)] * 3 + [tok(D_MODEL),
                                                        pl.BlockSpec((N_GROUPS, GROUP_W), lambda b, i: (0, 0)),
                                                        pl.BlockSpec((D_MODEL, D_MODEL), lambda b, i: (0, 0))],
        out_specs=tok(D_MODEL),
        out_shape=jax.ShapeDtypeStruct((t, D_MODEL), F32),
        scratch_shapes=[pltpu.VMEM((GROUP_W // LANES, tm, LANES), F32) for _ in range(4)],
        compiler_params=_cparams(2),
        name="outproj",
    )(*args)


def _mlp_kernel(x_ref, g_ref, wu_ref, wd_ref, gf_ref, o_ref, *, final, ff_chunk):
    x = x_ref[...]
    h = _rms(x, g_ref[...]).astype(BF16)
    acc = x
    for c in range(D_FF // ff_chunk):
        cols = slice(c * ff_chunk, (c + 1) * ff_chunk)
        u = jnp.dot(h, wu_ref[:, cols], preferred_element_type=F32)
        u = jnp.square(jnp.maximum(u, 0.0)).astype(BF16)
        acc = acc + jnp.dot(u, wd_ref[cols, :], preferred_element_type=F32)
    if final:
        acc = _rms(acc, gf_ref[...])
    o_ref[...] = acc


def _mlp(x2, g, wu, wd, gf, final, tm=512, ff_chunk=1024):
    t = x2.shape[0]
    return pl.pallas_call(
        functools.partial(_mlp_kernel, final=final, ff_chunk=ff_chunk),
        grid=(t // tm,),
        in_specs=[pl.BlockSpec((tm, D_MODEL), lambda i: (i, 0)),
                  pl.BlockSpec((1, D_MODEL), lambda i: (0, 0)),
                  pl.BlockSpec((D_MODEL, D_FF), lambda i: (0, 0)),
                  pl.BlockSpec((D_FF, D_MODEL), lambda i: (0, 0)),
                  pl.BlockSpec((1, D_MODEL), lambda i: (0, 0))],
        out_specs=pl.BlockSpec((tm, D_MODEL), lambda i: (i, 0)),
        out_shape=jax.ShapeDtypeStruct((t, D_MODEL), F32),
        compiler_params=_cparams(1),
        name="mlp_final" if final else "mlp",
    )(x2, g, wu, wd, gf)


def _gqa_perm():
    return np.concatenate([np.arange(h * HEAD_DIM, (h + 1) * HEAD_DIM) for h in GQA_HEAD_ORDER])


def _layer_weights(lp):
    (norm_attn, w_in, mla_q_norm, w_q_up, mla_kv_norm, w_kv_up, sink_logits, na_rpb,
     group_norm, w_out, norm_mlp, w_mlp_up, w_mlp_down) = lp
    perm = _gqa_perm()
    half = C_ROPE // 2
    z = lambda n: jnp.zeros((D_MODEL, n), w_in.dtype)
    kr = w_in[:, 1664:1696]
    kr_sw = jnp.concatenate([kr[:, half:], kr[:, :half]], axis=1)
    pad = MLA_HEAD_PAD - C_NOPE - C_ROPE
    qscale = HEAD_DIM ** -0.5 * LOG2E
    w_in2 = jnp.concatenate([
        w_in[:, 0:256] * qscale, w_in[:, 256:768],
        w_in[:, 768:1024][:, perm] * qscale, w_in[:, 1024:1152],
        w_in[:, 1696:1952] * qscale, w_in[:, 1952:2208],
        w_in[:, 1280:1664],
        z(C_NOPE), kr, z(pad), z(C_NOPE), kr_sw, z(pad),
    ], axis=1).astype(BF16)

    wq = w_q_up.reshape(C_Q_RANK, C_HEADS, C_NOPE + C_ROPE)
    zq = lambda n: jnp.zeros((C_Q_RANK, C_HEADS, n), w_q_up.dtype)
    rope = wq[:, :, C_NOPE:]
    rope_sw = jnp.concatenate([rope[:, :, half:], rope[:, :, :half]], axis=2)
    wq_main = jnp.concatenate([wq, zq(pad)], axis=2).reshape(C_Q_RANK, -1)
    wq_swap = jnp.concatenate([zq(C_NOPE), rope_sw, zq(pad)], axis=2).reshape(C_Q_RANK, -1)
    wq2 = jnp.concatenate([wq_main, wq_swap], axis=1).astype(BF16)

    wkv = w_kv_up.reshape(C_KV_RANK, C_HEADS, C_NOPE + HEAD_DIM)
    wk2 = jnp.concatenate([wkv[:, :, :C_NOPE], jnp.zeros((C_KV_RANK, C_HEADS, MLA_HEAD_PAD - C_NOPE), w_kv_up.dtype)],
                          axis=2).reshape(C_KV_RANK, -1).astype(BF16)
    wvt = wkv[:, :, C_NOPE:].reshape(C_KV_RANK, -1).T.astype(BF16)

    wvt_bd = jnp.concatenate([w_in[:, 1152:1280], w_in[:, 2208:2464]], axis=1).T.astype(BF16)
    return dict(
        norm_attn=norm_attn.reshape(1, -1), w_in=w_in2, wvt_bd=wvt_bd,
        qn=mla_q_norm.reshape(1, -1), kvn=mla_kv_norm.reshape(1, -1), wq=wq2, wk=wk2, wvt=wvt,
        sink=sink_logits[jnp.asarray(GQA_HEAD_ORDER)], rpb=na_rpb,
        gn=group_norm, w_out=w_out.astype(BF16), norm_mlp=norm_mlp.reshape(1, -1),
        w_up=w_mlp_up.astype(BF16), w_down=w_mlp_down.astype(BF16))


def _layer(x2, bsz, seq, lw, tables, gf, final):
    ct, st = tables
    qkv_a, qkv_a4, qkv_a16, qk_b, qk_d, c, vt_b, vt_d = _inproj(
        x2, lw["norm_attn"], lw["w_in"], lw["wvt_bd"], bsz, seq)
    a_outs = _mixer_a((qkv_a, qkv_a4, qkv_a16), seq)
    ob = _mixer_b(qk_b, vt_b, lw["sink"], seq)
    q, k, vt = _mla_prep(c.reshape(bsz, seq, W_C), ct, st, lw["qn"], lw["kvn"], lw["wq"], lw["wk"], lw["wvt"])
    oc = _mla_flash(q, k, vt).reshape(bsz * seq, GROUP_W)
    od = _mixer_d(qk_d, vt_d, lw["rpb"], seq)
    x2 = _outproj(a_outs, ob, oc, od, x2, lw["gn"], lw["w_out"], bsz, seq)
    return _mlp(x2, lw["norm_mlp"], lw["w_up"], lw["w_down"], gf, final)


def _trunk(x, layer_ws, gf):
    bsz, seq, _ = x.shape
    tables = _rope_tables(seq)
    x2 = x.reshape(bsz * seq, D_MODEL)
    for i, lw in enumerate(layer_ws):
        x2 = _layer(x2, bsz, seq, lw, tables, gf, final=(i == len(layer_ws) - 1))
    return x2.reshape(bsz, seq, D_MODEL)


def kernel(x_prompt, x_sample, norm_attn, w_in, mla_q_norm, w_q_up, mla_kv_norm, w_kv_up, sink_logits, na_rpb, group_norm, w_out, norm_mlp, w_mlp_up, w_mlp_down, norm_final):
    params = (norm_attn, w_in, mla_q_norm, w_q_up, mla_kv_norm, w_kv_up, sink_logits, na_rpb,
              group_norm, w_out, norm_mlp, w_mlp_up, w_mlp_down)
    depth = norm_attn.shape[0]
    layer_ws = [_layer_weights(tuple(p[i] for p in params)) for i in range(depth)]
    gf = norm_final.reshape(1, -1)
    return (_trunk(x_prompt, layer_ws, gf), _trunk(x_sample, layer_ws, gf))
```

```python
import functools

import numpy as np
import jax
import jax.numpy as jnp
from jax import lax
from jax.experimental import pallas as pl
from jax.experimental.pallas import tpu as pltpu

F32 = jnp.float32
BF16 = jnp.bfloat16

D_MODEL = 1024
HEAD_DIM = 64
N_GROUPS = 4
GROUP_W = 256
A_CONFIGS = ((128, 1), (512, 4), (2048, 16))
B_HALF_WINDOW = 128
C_Q_RANK = 256
C_KV_RANK = 128
C_NOPE = 64
C_ROPE = 32
C_HEADS = 4
ROPE_THETA = 10000.0
GRID_W = 64
NA_KH = 8
NA_KW = 16
D_FF = 4096
EPS = 1e-5
NEG_INF = -1e30

W_A = 768
W_B = 384
W_D = 512
W_VT = 384
W_C = 640
MLA_HEAD_PAD = 128
MLA_V_ROWS = 80
LOG2E = 1.4426950408889634
LANES = 128

VMEM_LIMIT = 56 * 1024 * 1024


def _cparams(n_axes):
    return pltpu.CompilerParams(dimension_semantics=("arbitrary",) * n_axes,
                                vmem_limit_bytes=VMEM_LIMIT)


def _rms(x, g):
    return x * lax.rsqrt(jnp.mean(x * x, axis=-1, keepdims=True) + EPS) * g


def _mla_prep(c, ct, st, qn, kvn, wq_ref, wk_ref, wvt_ref, q_ref, k_ref, vt_ref):
    hp = MLA_HEAD_PAD
    cqn = _rms(c[:, :C_Q_RANK], qn).astype(BF16)
    qq = jnp.dot(cqn, wq_ref[...], preferred_element_type=F32)
    scale = (C_NOPE + C_ROPE) ** -0.5 * LOG2E
    q_ref[0] = jnp.concatenate(
        [(qq[:, h * hp:(h + 1) * hp] * ct + qq[:, (C_HEADS + h) * hp:(C_HEADS + h + 1) * hp] * st) * scale
         for h in range(C_HEADS)], axis=1).astype(q_ref.dtype)
    ckvn = _rms(c[:, C_Q_RANK:C_Q_RANK + C_KV_RANK], kvn).astype(BF16)
    kk = jnp.dot(ckvn, wk_ref[...], preferred_element_type=F32)
    o0 = C_Q_RANK + C_KV_RANK
    kpe = c[:, o0:o0 + hp] * ct + c[:, o0 + hp:o0 + 2 * hp] * st
    k_ref[0] = jnp.concatenate([kk[:, h * hp:(h + 1) * hp] + kpe for h in range(C_HEADS)],
                               axis=1).astype(k_ref.dtype)
    vt = lax.dot_general(wvt_ref[...], ckvn, (((1,), (1,)), ((), ())), preferred_element_type=F32)
    ones = jnp.ones((MLA_V_ROWS - HEAD_DIM, vt.shape[1]), F32)
    vt_ref[0] = jnp.concatenate(
        [blk for h in range(C_HEADS) for blk in (vt[h * HEAD_DIM:(h + 1) * HEAD_DIM], ones)],
        axis=0).astype(vt_ref.dtype)


def _inproj_kernel(x_ref, g_ref, w_ref, wvt_ref, ct_ref, st_ref, qn_ref, kvn_ref, wq_ref, wk_ref, wcvt_ref,
                   oa_ref, oa4_ref, oa16_ref, ob_ref, od_ref, vtb_ref, vtd_ref, cq_ref, ck_ref, cvt_ref, a_sc):
    h = _rms(x_ref[...], g_ref[...]).astype(BF16)
    tm = h.shape[0]
    qkv_a = jnp.dot(h, w_ref[:, :W_A], preferred_element_type=F32)
    oa_ref[...] = qkv_a.astype(oa_ref.dtype)
    for c in range(W_A // LANES):
        a_sc[c] = qkv_a[:, c * LANES:(c + 1) * LANES]
    for dil, o_ref in ((4, oa4_ref), (16, oa16_ref)):
        for r in range(dil):
            o_ref[0, r] = jnp.concatenate(
                [a_sc[c, pl.ds(r, tm // dil, stride=dil), :] for c in range(W_A // LANES)],
                axis=1).astype(o_ref.dtype)
    off = W_A
    for o_ref in (ob_ref, od_ref):
        n = o_ref.shape[-1]
        o_ref[...] = jnp.dot(h, w_ref[:, off:off + n], preferred_element_type=F32).astype(o_ref.dtype)
        off += n
    c = jnp.dot(h, w_ref[:, off:off + W_C], preferred_element_type=F32)
    _mla_prep(c, ct_ref[...], st_ref[...], qn_ref[...], kvn_ref[...], wq_ref, wk_ref, wcvt_ref,
              cq_ref, ck_ref, cvt_ref)
    vt = lax.dot_general(wvt_ref[...], h, (((1,), (1,)), ((), ())), preferred_element_type=F32)
    nb = vtb_ref.shape[0]
    vtb_ref[...] = vt[:nb].astype(vtb_ref.dtype)
    vtd_ref[...] = vt[nb:].astype(vtd_ref.dtype)


def _inproj(x2, lw, tables, bsz, seq, tm=512):
    t = x2.shape[0]
    nt = seq // tm
    widths = (W_B, W_D)
    vrows = (W_VT - GROUP_W, GROUP_W)
    hw = C_HEADS * MLA_HEAD_PAD
    cvrows = C_HEADS * MLA_V_ROWS
    tok = lambda n: pl.BlockSpec((tm, n), lambda b, i: (b * nt + i, 0))
    plane = lambda dil: pl.BlockSpec((1, dil, tm // dil, W_A), lambda b, i: (b, 0, i, 0))
    const = lambda shape: pl.BlockSpec(shape, lambda b, i: (0,) * len(shape))
    pos = pl.BlockSpec((tm, MLA_HEAD_PAD), lambda b, i: (i, 0))
    return pl.pallas_call(
        _inproj_kernel,
        grid=(bsz, nt),
        in_specs=[tok(D_MODEL), const((1, D_MODEL)), const((D_MODEL, W_A + sum(widths) + W_C)),
                  const((W_VT, D_MODEL)), pos, pos, const((1, C_Q_RANK)), const((1, C_KV_RANK)),
                  const((C_Q_RANK, 2 * hw)), const((C_KV_RANK, hw)), const((GROUP_W, C_KV_RANK))],
        out_specs=([tok(W_A), plane(4), plane(16)] + [tok(n) for n in widths]
                   + [pl.BlockSpec((r, tm), lambda b, i: (0, b * nt + i)) for r in vrows]
                   + [pl.BlockSpec((1, tm, hw), lambda b, i: (b, i, 0)),
                      pl.BlockSpec((1, tm, hw), lambda b, i: (b, i, 0)),
                      pl.BlockSpec((1, cvrows, tm), lambda b, i: (b, 0, i))]),
        out_shape=([jax.ShapeDtypeStruct((t, W_A), BF16)]
                   + [jax.ShapeDtypeStruct((bsz, dil, seq // dil, W_A), BF16) for dil in (4, 16)]
                   + [jax.ShapeDtypeStruct((t, n), BF16) for n in widths]
                   + [jax.ShapeDtypeStruct((r, t), BF16) for r in vrows]
                   + [jax.ShapeDtypeStruct((bsz, seq, hw), BF16), jax.ShapeDtypeStruct((bsz, seq, hw), BF16),
                      jax.ShapeDtypeStruct((bsz, cvrows, seq), BF16)]),
        scratch_shapes=[pltpu.VMEM((W_A // LANES, tm, LANES), F32)],
        compiler_params=_cparams(2),
        name="inproj",
    )(x2, lw["norm_attn"], lw["w_in"], lw["wvt_bd"], tables[0], tables[1], lw["qn"], lw["kvn"],
      lw["wq"], lw["wk"], lw["wvt"])


def _win_attn_t_kernel(*refs, heads, ck, tq, hb, n_tiles, has_sink, has_lse, v_rows):
    q_ref, kp_ref, km_ref, kn_ref, vp_ref, vm_ref, vn_ref, bias_ref = refs[:8]
    sink_ref = refs[8] if has_sink else None
    o_ref = refs[-2] if has_lse else refs[-1]
    lse_ref = refs[-1] if has_lse else None
    ts = q_ref.shape[0]
    tkw = tq + 2 * hb
    nh = len(heads)
    kw = jnp.concatenate([kp_ref[...], km_ref[...], kn_ref[...]], axis=0)
    if v_rows:
        vw = jnp.concatenate([vp_ref[...], vm_ref[...], vn_ref[...]], axis=0).astype(F32).T.astype(BF16)
    else:
        vw = jnp.concatenate([vp_ref[...], vm_ref[...], vn_ref[...]], axis=1)
    lane = lax.broadcasted_iota(jnp.int32, (tq, ck), 1)
    masks = [(lane >= lo) & (lane < lo + HEAD_DIM) for (_, lo, _, _) in heads]
    ones = jnp.ones((MLA_V_ROWS - HEAD_DIM, tkw), vw.dtype)
    step = pl.program_id(0)
    for j in range(ts // tq):
        tile = (step * (ts // tq) + j) % n_tiles
        kind = jnp.where(tile == 0, 0, jnp.where(tile == n_tiles - 1, 2, 1))
        q = q_ref[j * tq:(j + 1) * tq, :]
        q_stack = jnp.concatenate(
            [jnp.where(mk, q[:, g * ck:(g + 1) * ck], jnp.zeros((), q.dtype))
             for (g, _, _, _), mk in zip(heads, masks)], axis=0)
        st = lax.dot_general(kw[j * tq:j * tq + tkw], q_stack, (((1,), (1,)), ((), ())),
                             preferred_element_type=F32) + bias_ref[kind]
        m = jnp.max(st, axis=0, keepdims=True)
        if has_sink:
            sink = jnp.concatenate([jnp.full((1, tq), sink_ref[i] * LOG2E, F32) for i in range(nh)], axis=1)
            m = jnp.maximum(m, sink)
            sink_p = jnp.exp2(sink - m)
        pt = jnp.exp2(st - m).astype(BF16)
        outs = [None] * nh
        lses = [None] * nh
        for i, (_, _, vrow, out_pos) in enumerate(heads):
            lhs = jnp.concatenate([vw[vrow:vrow + HEAD_DIM, j * tq:j * tq + tkw], ones], axis=0)
            r = jnp.dot(lhs, pt[:, i * tq:(i + 1) * tq], preferred_element_type=F32)
            den = r[HEAD_DIM:HEAD_DIM + 1]
            if has_sink:
                den = den + sink_p[:, i * tq:(i + 1) * tq]
            outs[out_pos] = r[:HEAD_DIM] / den
            if has_lse:
                lse = (m[:, i * tq:(i + 1) * tq] + jnp.log2(den)) * (1.0 / LOG2E)
                lses[out_pos] = jnp.broadcast_to(lse, (HEAD_DIM, tq))
        o_ref[j * tq:(j + 1) * tq, :] = jnp.concatenate(outs, axis=0).T.astype(o_ref.dtype)
        if has_lse:
            lse_ref[j * tq:(j + 1) * tq, :] = jnp.concatenate(lses, axis=0).T


def _win_attn_t(qk, vt, bias, sink, *, seq, ts, tq, hb, cq, ck, qcol, kcol, vcol, heads, has_lse, name):
    rows = qk.shape[0]
    ts = min(ts, rows)
    tq = min(tq, seq)
    per = ts // hb
    nhb = rows // hb
    nh = len(heads)
    tkw = tq + 2 * hb
    v_rows = vt is None

    def prev_idx(i):
        return jnp.maximum(i * per - 1, 0)

    def next_idx(i):
        return jnp.minimum((i + 1) * per, nhb - 1)

    in_specs = [
        pl.BlockSpec((ts, cq), lambda i: (i, qcol)),
        pl.BlockSpec((hb, ck), lambda i: (prev_idx(i), kcol)),
        pl.BlockSpec((ts, ck), lambda i: (i, kcol)),
        pl.BlockSpec((hb, ck), lambda i: (next_idx(i), kcol)),
    ]
    if v_rows:
        in_specs += [
            pl.BlockSpec((hb, GROUP_W), lambda i: (prev_idx(i), vcol)),
            pl.BlockSpec((ts, GROUP_W), lambda i: (i, vcol)),
            pl.BlockSpec((hb, GROUP_W), lambda i: (next_idx(i), vcol)),
        ]
        args = [qk] * 7
    else:
        cv = vt.shape[0]
        in_specs += [
            pl.BlockSpec((cv, hb), lambda i: (0, prev_idx(i))),
            pl.BlockSpec((cv, ts), lambda i: (0, i)),
            pl.BlockSpec((cv, hb), lambda i: (0, next_idx(i))),
        ]
        args = [qk] * 4 + [vt] * 3
    in_specs.append(pl.BlockSpec((3, tkw, nh * tq), lambda i: (0, 0, 0)))
    args.append(bias)
    if sink is not None:
        in_specs.append(pl.BlockSpec(memory_space=pltpu.SMEM))
        args.append(sink)
    out_specs = [pl.BlockSpec((ts, GROUP_W), lambda i: (i, 0))]
    out_shape = [jax.ShapeDtypeStruct((rows, GROUP_W), BF16)]
    if has_lse:
        out_specs.append(pl.BlockSpec((ts, GROUP_W), lambda i: (i, 0)))
        out_shape.append(jax.ShapeDtypeStruct((rows, GROUP_W), F32))
    return pl.pallas_call(
        functools.partial(_win_attn_t_kernel, heads=heads, ck=ck, tq=tq, hb=hb, n_tiles=seq // tq,
                          has_sink=sink is not None, has_lse=has_lse, v_rows=v_rows),
        grid=(rows // ts,),
        in_specs=in_specs,
        out_specs=out_specs,
        out_shape=out_shape,
        compiler_params=_cparams(1),
        name=name,
    )(*args)


def _transpose_bias(bias):
    return jnp.swapaxes(bias, 1, 2) * LOG2E


def _tile_indices(nt):
    return (0, min(1, nt - 1), nt - 1)


def _banded_bias(seq, tq, hb, hw, slopes, dist_scale):
    nt = seq // tq
    tabs = []
    for ti in _tile_indices(nt):
        qpos = ti * tq + np.arange(tq)[:, None]
        kpos = ti * tq - hb + np.arange(tq + 2 * hb)[None, :]
        dist = np.abs(kpos - qpos)
        valid = (dist <= hw) & (kpos >= 0) & (kpos < seq)
        pen = -np.asarray(slopes, np.float32)[:, None, None] * (dist.astype(np.float32) * np.float32(dist_scale))[None]
        tabs.append(np.where(valid[None], pen, np.float32(NEG_INF)).reshape(-1, tq + 2 * hb))
    return jnp.asarray(np.stack(tabs).astype(np.float32))


def _na_bias(rpb, seq, rows_per_tile):
    rows = seq // GRID_W
    kh = min(NA_KH, rows)
    nt = rows // rows_per_tile
    tq = rows_per_tile * GRID_W
    cq = np.arange(GRID_W)[:, None]
    ck = np.arange(GRID_W)[None, :]
    col_start = np.clip(cq - NA_KW // 2, 0, GRID_W - NA_KW)
    col_ok = (ck >= col_start) & (ck < col_start + NA_KW)
    dc = np.clip(ck - cq, -(NA_KW - 1), NA_KW - 1) + (NA_KW - 1)
    col_hot = jnp.asarray(np.eye(2 * NA_KW - 1, dtype=np.float32)[dc])
    tabs = []
    for ti in _tile_indices(nt):
        rq = (ti * rows_per_tile + np.arange(rows_per_tile))[:, None]
        rk = ((ti - 1) * rows_per_tile + np.arange(3 * rows_per_tile))[None, :]
        row_start = np.clip(rq - kh // 2, 0, rows - kh)
        row_ok = (rk >= row_start) & (rk < row_start + kh)
        dr = np.clip(rk - rq + (NA_KH - 1), 0, 2 * NA_KH - 2)
        row_hot = jnp.asarray(np.eye(2 * NA_KH - 1, dtype=np.float32)[dr])
        vals = jnp.einsum("abr,hrc,xyc->haxby", row_hot, rpb.astype(F32), col_hot,
                          precision=lax.Precision.HIGHEST)
        valid = row_ok[:, None, :, None] & col_ok[None, :, None, :]
        tabs.append(jnp.where(valid[None], vals, NEG_INF).reshape(-1, 3 * tq))
    return jnp.stack(tabs)


def _alibi_slopes():
    s = np.exp2(-8.0 * np.arange(1, 9, dtype=np.float32) / 8.0)
    return s[1::2], s[0::2]


GQA_HEAD_ORDER = (0, 2, 1, 3)
HEADS_T_FULL = tuple((0, 64 * h, 64 * h, h) for h in range(4))
HEADS_T_GQA = ((0, 0, 0, 0), (0, 64, 64, 2), (1, 0, 0, 1), (1, 64, 64, 3))

A_HALO = 64


def _mixer_a(qkv_planes, seq):
    slopes_a, _ = _alibi_slopes()
    outs = []
    for (window, dil), qkv in zip(A_CONFIGS, qkv_planes):
        length = seq // dil
        hw = window // (2 * dil)
        tq = min(128, length)
        bias = _banded_bias(length, tq, A_HALO, hw, slopes_a, float(dil))
        outs.append(_win_attn_t(
            qkv.reshape(-1, W_A), None, _transpose_bias(bias), None, seq=length, ts=1024, tq=tq, hb=A_HALO,
            cq=GROUP_W, ck=GROUP_W, qcol=0, kcol=1, vcol=2, heads=HEADS_T_FULL, has_lse=True,
            name=f"mix_a_d{dil}"))
    return outs


def _mixer_b(qk_b, vt_b, sink, seq):
    _, slopes_b = _alibi_slopes()
    tq = 256
    bias = _banded_bias(seq, tq, B_HALF_WINDOW, B_HALF_WINDOW, slopes_b[list(GQA_HEAD_ORDER)], 1.0)
    (o,) = _win_attn_t(
        qk_b, vt_b, _transpose_bias(bias), sink, seq=seq, ts=1024, tq=tq, hb=B_HALF_WINDOW,
        cq=GROUP_W, ck=128, qcol=0, kcol=2, vcol=None, heads=HEADS_T_GQA, has_lse=False, name="mix_b")
    return o


def _mixer_d(qk_d, vt_d, rpb, seq):
    rows_per_tile = 4
    tq = rows_per_tile * GRID_W
    bias = _na_bias(rpb, seq, rows_per_tile)
    (o,) = _win_attn_t(
        qk_d, vt_d, _transpose_bias(bias), None, seq=seq, ts=1024, tq=tq, hb=tq,
        cq=GROUP_W, ck=GROUP_W, qcol=0, kcol=1, vcol=None, heads=HEADS_T_FULL, has_lse=False, name="mix_d")
    return o


def _mla_flash_kernel(q_ref, k_ref, vt_ref, o_ref, m_sc, acc_sc, s_sc, *, kc, depth):
    hp, hv = MLA_HEAD_PAD, MLA_V_ROWS
    kv = pl.program_id(2)
    tk = k_ref.shape[1]

    @pl.when(kv == 0)
    def _():
        m_sc[...] = jnp.full(m_sc.shape, NEG_INF, F32)
        acc_sc[...] = jnp.zeros(acc_sc.shape, F32)

    units = [(c, h) for c in range(tk // kc) for h in range(C_HEADS)]

    def scores(u):
        c, h = units[u]
        s_sc[u % depth] = lax.dot_general(k_ref[0, c * kc:(c + 1) * kc, h * hp:(h + 1) * hp],
                                          q_ref[0, :, h * hp:(h + 1) * hp],
                                          (((1,), (1,)), ((), ())), preferred_element_type=F32)

    for u in range(min(depth, len(units))):
        scores(u)
    for u, (c, h) in enumerate(units):
        st = s_sc[u % depth]
        m_old = m_sc[h:h + 1, :]
        m_new = jnp.maximum(m_old, jnp.max(st, axis=0, keepdims=True))
        alpha = jnp.exp2(m_old - m_new)
        pt = jnp.exp2(st - m_new).astype(BF16)
        m_sc[h:h + 1, :] = m_new
        if u + depth < len(units):
            scores(u + depth)
        rows = slice(h * hv, (h + 1) * hv)
        acc_sc[rows, :] = alpha * acc_sc[rows, :] + jnp.dot(
            vt_ref[0, rows, c * kc:(c + 1) * kc], pt, preferred_element_type=F32)

    @pl.when(kv == pl.num_programs(2) - 1)
    def _():
        ot = jnp.concatenate(
            [acc_sc[h * hv:h * hv + HEAD_DIM, :] / acc_sc[h * hv + HEAD_DIM:h * hv + HEAD_DIM + 1, :]
             for h in range(C_HEADS)], axis=0)
        o_ref[0] = ot.T.astype(o_ref.dtype)


def _mla_flash(q, k, vt, tq=512, tk=4096, kc=256, depth=8):
    b, s, hw = q.shape
    tq = min(tq, s)
    tk = min(tk, s)
    vrows = C_HEADS * MLA_V_ROWS
    return pl.pallas_call(
        functools.partial(_mla_flash_kernel, kc=kc, depth=depth),
        grid=(b, s // tq, s // tk),
        in_specs=[pl.BlockSpec((1, tq, hw), lambda n, i, j: (n, i, 0)),
                  pl.BlockSpec((1, tk, hw), lambda n, i, j: (n, j, 0)),
                  pl.BlockSpec((1, vrows, tk), lambda n, i, j: (n, 0, j))],
        out_specs=pl.BlockSpec((1, tq, GROUP_W), lambda n, i, j: (n, i, 0)),
        out_shape=jax.ShapeDtypeStruct((b, s, GROUP_W), BF16),
        scratch_shapes=[pltpu.VMEM((8, tq), F32), pltpu.VMEM((vrows, tq), F32),
                        pltpu.VMEM((depth, kc, tq), F32)],
        compiler_params=_cparams(3),
        name="mla_flash",
    )(q, k, vt)


def _rope_tables(seq):
    inv_freq = 1.0 / (ROPE_THETA ** (jnp.arange(0, C_ROPE, 2, dtype=F32) / C_ROPE))
    ang = jnp.arange(seq, dtype=F32)[:, None] * inv_freq[None, :]
    cos, sin = jnp.cos(ang), jnp.sin(ang)
    pad = MLA_HEAD_PAD - C_NOPE - C_ROPE
    ct = jnp.concatenate([jnp.ones((seq, C_NOPE), F32), cos, cos, jnp.zeros((seq, pad), F32)], axis=1)
    st = jnp.concatenate([jnp.zeros((seq, C_NOPE), F32), -sin, sin, jnp.zeros((seq, pad), F32)], axis=1)
    return ct, st


def _outproj_kernel(a1_ref, a2_ref, a3_ref, l1_ref, l2_ref, l3_ref, ob_ref, oc_ref, od_ref,
                    x_ref, g_ref, w_ref, gm_ref, wu_ref, wd_ref, gf_ref, o_ref, *scratch, final, ff_chunk):
    tm = x_ref.shape[0]
    ncol = GROUP_W // LANES

    def natural(ref, dil, sc):
        for r in range(dil):
            for c in range(ncol):
                sc[c, pl.ds(r, tm // dil, stride=dil), :] = ref[0, r, :, c * LANES:(c + 1) * LANES].astype(F32)
        return jnp.concatenate([sc[c] for c in range(ncol)], axis=1)

    a_s = [a1_ref[...].astype(F32), natural(a2_ref, 4, scratch[0]), natural(a3_ref, 16, scratch[1])]
    ls = [l1_ref[...], natural(l2_ref, 4, scratch[2]), natural(l3_ref, 16, scratch[3])]
    m = jnp.maximum(jnp.maximum(ls[0], ls[1]), ls[2])
    es = [jnp.exp(l - m) for l in ls]
    den = es[0] + es[1] + es[2]
    oa = sum((e / den) * a for e, a in zip(es, a_s))
    groups = [oa, ob_ref[...].astype(F32), oc_ref[...].astype(F32), od_ref[...].astype(F32)]
    y = jnp.concatenate([_rms(o, g_ref[i:i + 1, :]).astype(BF16) for i, o in enumerate(groups)], axis=1)
    x = x_ref[...] + jnp.dot(y, w_ref[...], preferred_element_type=F32)
    h = _rms(x, gm_ref[...]).astype(BF16)
    acc = x
    for c in range(D_FF // ff_chunk):
        cols = slice(c * ff_chunk, (c + 1) * ff_chunk)
        u = jnp.dot(h, wu_ref[:, cols], preferred_element_type=F32)
        u = jnp.square(jnp.maximum(u, 0.0)).astype(BF16)
        acc = acc + jnp.dot(u, wd_ref[cols, :], preferred_element_type=F32)
    if final:
        acc = _rms(acc, gf_ref[...])
    o_ref[...] = acc


def _outproj_mlp(a_outs, ob, oc, od, x2, lw, gf, final, bsz, seq, tm=512, ff_chunk=1024):
    t = x2.shape[0]
    nt = seq // tm
    tok = lambda n: pl.BlockSpec((tm, n), lambda b, i: (b * nt + i, 0))
    plane = lambda dil: pl.BlockSpec((1, dil, tm // dil, GROUP_W), lambda b, i: (b, 0, i, 0))
    dils = [dil for _, dil in A_CONFIGS]
    a_specs = [tok(GROUP_W) if dil == 1 else plane(dil) for dil in dils]
    planes = lambda arr, dil: arr if dil == 1 else arr.reshape(bsz, dil, seq // dil, GROUP_W)
    args = ([planes(o, dil) for (o, _), dil in zip(a_outs, dils)] + [planes(l, dil) for (_, l), dil in zip(a_outs, dils)]
            + [ob, oc, od, x2, lw["gn"], lw["w_out"], lw["norm_mlp"], lw["w_up"], lw["w_down"], gf])
    const = lambda shape: pl.BlockSpec(shape, lambda b, i: (0,) * len(shape))
    return pl.pallas_call(
        functools.partial(_outproj_kernel, final=final, ff_chunk=ff_chunk),
        grid=(bsz, nt),
        in_specs=a_specs + a_specs + [tok(GROUP_W)] * 3 + [
            tok(D_MODEL), const((N_GROUPS, GROUP_W)), const((D_MODEL, D_MODEL)),
            const((1, D_MODEL)), const((D_MODEL, D_FF)), const((D_FF, D_MODEL)), const((1, D_MODEL))],
        out_specs=tok(D_MODEL),
        out_shape=jax.ShapeDtypeStruct((t, D_MODEL), F32),
        scratch_shapes=[pltpu.VMEM((GROUP_W // LANES, tm, LANES), F32) for _ in range(4)],
        compiler_params=_cparams(2),
        name="outproj_mlp_final" if final else "outproj_mlp",
    )(*args)


def _gqa_perm():
    return np.concatenate([np.arange(h * HEAD_DIM, (h + 1) * HEAD_DIM) for h in GQA_HEAD_ORDER])


def _layer_weights(lp):
    (norm_attn, w_in, mla_q_norm, w_q_up, mla_kv_norm, w_kv_up, sink_logits, na_rpb,
     group_norm, w_out, norm_mlp, w_mlp_up, w_mlp_down) = lp
    perm = _gqa_perm()
    half = C_ROPE // 2
    z = lambda n: jnp.zeros((D_MODEL, n), w_in.dtype)
    kr = w_in[:, 1664:1696]
    kr_sw = jnp.concatenate([kr[:, half:], kr[:, :half]], axis=1)
    pad = MLA_HEAD_PAD - C_NOPE - C_ROPE
    qscale = HEAD_DIM ** -0.5 * LOG2E
    w_in2 = jnp.concatenate([
        w_in[:, 0:256] * qscale, w_in[:, 256:768],
        w_in[:, 768:1024][:, perm] * qscale, w_in[:, 1024:1152],
        w_in[:, 1696:1952] * qscale, w_in[:, 1952:2208],
        w_in[:, 1280:1664],
        z(C_NOPE), kr, z(pad), z(C_NOPE), kr_sw, z(pad),
    ], axis=1).astype(BF16)

    wq = w_q_up.reshape(C_Q_RANK, C_HEADS, C_NOPE + C_ROPE)
    zq = lambda n: jnp.zeros((C_Q_RANK, C_HEADS, n), w_q_up.dtype)
    rope = wq[:, :, C_NOPE:]
    rope_sw = jnp.concatenate([rope[:, :, half:], rope[:, :, :half]], axis=2)
    wq_main = jnp.concatenate([wq, zq(pad)], axis=2).reshape(C_Q_RANK, -1)
    wq_swap = jnp.concatenate([zq(C_NOPE), rope_sw, zq(pad)], axis=2).reshape(C_Q_RANK, -1)
    wq2 = jnp.concatenate([wq_main, wq_swap], axis=1).astype(BF16)

    wkv = w_kv_up.reshape(C_KV_RANK, C_HEADS, C_NOPE + HEAD_DIM)
    wk2 = jnp.concatenate([wkv[:, :, :C_NOPE], jnp.zeros((C_KV_RANK, C_HEADS, MLA_HEAD_PAD - C_NOPE), w_kv_up.dtype)],
                          axis=2).reshape(C_KV_RANK, -1).astype(BF16)
    wvt = wkv[:, :, C_NOPE:].reshape(C_KV_RANK, -1).T.astype(BF16)

    wvt_bd = jnp.concatenate([w_in[:, 1152:1280], w_in[:, 2208:2464]], axis=1).T.astype(BF16)
    return dict(
        norm_attn=norm_attn.reshape(1, -1), w_in=w_in2, wvt_bd=wvt_bd,
        qn=mla_q_norm.reshape(1, -1), kvn=mla_kv_norm.reshape(1, -1), wq=wq2, wk=wk2, wvt=wvt,
        sink=sink_logits[jnp.asarray(GQA_HEAD_ORDER)], rpb=na_rpb,
        gn=group_norm, w_out=w_out.astype(BF16), norm_mlp=norm_mlp.reshape(1, -1),
        w_up=w_mlp_up.astype(BF16), w_down=w_mlp_down.astype(BF16))


def _layer(x2, bsz, seq, lw, tables, gf, final):
    qkv_a, qkv_a4, qkv_a16, qk_b, qk_d, vt_b, vt_d, q, k, vt = _inproj(x2, lw, tables, bsz, seq)
    a_outs = _mixer_a((qkv_a, qkv_a4, qkv_a16), seq)
    ob = _mixer_b(qk_b, vt_b, lw["sink"], seq)
    oc = _mla_flash(q, k, vt).reshape(bsz * seq, GROUP_W)
    od = _mixer_d(qk_d, vt_d, lw["rpb"], seq)
    return _outproj_mlp(a_outs, ob, oc, od, x2, lw, gf, final, bsz, seq)


def _trunk(x, layer_ws, gf):
    bsz, seq, _ = x.shape
    tables = _rope_tables(seq)
    x2 = x.reshape(bsz * seq, D_MODEL)
    for i, lw in enumerate(layer_ws):
        x2 = _layer(x2, bsz, seq, lw, tables, gf, final=(i == len(layer_ws) - 1))
    return x2.reshape(bsz, seq, D_MODEL)


def kernel(x_prompt, x_sample, norm_attn, w_in, mla_q_norm, w_q_up, mla_kv_norm, w_kv_up, sink_logits, na_rpb, group_norm, w_out, norm_mlp, w_mlp_up, w_mlp_down, norm_final):
    params = (norm_attn, w_in, mla_q_norm, w_q_up, mla_kv_norm, w_kv_up, sink_logits, na_rpb,
              group_norm, w_out, norm_mlp, w_mlp_up, w_mlp_down)
    depth = norm_attn.shape[0]
    layer_ws = [_layer_weights(tuple(p[i] for p in params)) for i in range(depth)]
    gf = norm_final.reshape(1, -1)
    return (_trunk(x_prompt, layer_ws, gf), _trunk(x_sample, layer_ws, gf))
```

```python
import functools

import numpy as np
import jax
import jax.numpy as jnp
from jax import lax
from jax.experimental import pallas as pl
from jax.experimental.pallas import tpu as pltpu

F32 = jnp.float32
BF16 = jnp.bfloat16

D_MODEL = 1024
HEAD_DIM = 64
N_GROUPS = 4
GROUP_W = 256
A_CONFIGS = ((128, 1), (512, 4), (2048, 16))
B_HALF_WINDOW = 128
C_Q_RANK = 256
C_KV_RANK = 128
C_NOPE = 64
C_ROPE = 32
C_HEADS = 4
ROPE_THETA = 10000.0
GRID_W = 64
NA_KH = 8
NA_KW = 16
D_FF = 4096
EPS = 1e-5
NEG_INF = -1e30

W_A = 768
W_B = 384
W_D = 512
W_VT = 384
W_C = 640
MLA_HEAD_PAD = 128
MLA_V_ROWS = 80
MLA_KC = 256
LOG2E = 1.4426950408889634
LANES = 128

VMEM_LIMIT = 56 * 1024 * 1024


def _cparams(n_axes):
    return pltpu.CompilerParams(dimension_semantics=("arbitrary",) * n_axes,
                                vmem_limit_bytes=VMEM_LIMIT)


def _rms(x, g):
    return x * lax.rsqrt(jnp.mean(x * x, axis=-1, keepdims=True) + EPS) * g


def _mla_prep(c, ct, st, qn, kvn, wq_ref, wk_ref, wvt_ref, q_ref, k_ref, vt_ref):
    hp = MLA_HEAD_PAD
    cqn = _rms(c[:, :C_Q_RANK], qn).astype(BF16)
    qq = jnp.dot(cqn, wq_ref[...], preferred_element_type=F32)
    scale = (C_NOPE + C_ROPE) ** -0.5 * LOG2E
    q_ref[0] = jnp.concatenate(
        [(qq[:, h * hp:(h + 1) * hp] * ct + qq[:, (C_HEADS + h) * hp:(C_HEADS + h + 1) * hp] * st) * scale
         for h in range(C_HEADS)], axis=1).astype(q_ref.dtype)
    ckvn = _rms(c[:, C_Q_RANK:C_Q_RANK + C_KV_RANK], kvn).astype(BF16)
    kk = jnp.dot(ckvn, wk_ref[...], preferred_element_type=F32)
    o0 = C_Q_RANK + C_KV_RANK
    kpe = c[:, o0:o0 + hp] * ct + c[:, o0 + hp:o0 + 2 * hp] * st
    k_ref[0] = jnp.concatenate([kk[:, h * hp:(h + 1) * hp] + kpe for h in range(C_HEADS)],
                               axis=1).astype(k_ref.dtype)
    vt = lax.dot_general(wvt_ref[...], ckvn, (((1,), (1,)), ((), ())), preferred_element_type=F32)
    ones = jnp.ones((MLA_V_ROWS - HEAD_DIM, vt.shape[1]), F32)
    vt = jnp.concatenate(
        [blk for h in range(C_HEADS) for blk in (vt[h * HEAD_DIM:(h + 1) * HEAD_DIM], ones)],
        axis=0).astype(vt_ref.dtype)
    for cc in range(vt_ref.shape[1]):
        vt_ref[0, cc] = vt[:, cc * MLA_KC:(cc + 1) * MLA_KC]


def _inproj_kernel(x_ref, g_ref, w_ref, wvt_ref, ct_ref, st_ref, qn_ref, kvn_ref, wq_ref, wk_ref, wcvt_ref,
                   oa_ref, oa4_ref, oa16_ref, ob_ref, od_ref, vtb_ref, vtd_ref, cq_ref, ck_ref, cvt_ref, a_sc):
    h = _rms(x_ref[...], g_ref[...]).astype(BF16)
    tm = h.shape[0]
    qkv_a = jnp.dot(h, w_ref[:, :W_A], preferred_element_type=F32)
    oa_ref[...] = qkv_a.astype(oa_ref.dtype)
    for c in range(W_A // LANES):
        a_sc[c] = qkv_a[:, c * LANES:(c + 1) * LANES]
    for dil, o_ref in ((4, oa4_ref), (16, oa16_ref)):
        for r in range(dil):
            o_ref[0, r] = jnp.concatenate(
                [a_sc[c, pl.ds(r, tm // dil, stride=dil), :] for c in range(W_A // LANES)],
                axis=1).astype(o_ref.dtype)
    off = W_A
    for o_ref in (ob_ref, od_ref):
        n = o_ref.shape[-1]
        o_ref[...] = jnp.dot(h, w_ref[:, off:off + n], preferred_element_type=F32).astype(o_ref.dtype)
        off += n
    c = jnp.dot(h, w_ref[:, off:off + W_C], preferred_element_type=F32)
    _mla_prep(c, ct_ref[...], st_ref[...], qn_ref[...], kvn_ref[...], wq_ref, wk_ref, wcvt_ref,
              cq_ref, ck_ref, cvt_ref)
    vt = lax.dot_general(wvt_ref[...], h, (((1,), (1,)), ((), ())), preferred_element_type=F32)
    nb = vtb_ref.shape[0]
    vtb_ref[...] = vt[:nb].astype(vtb_ref.dtype)
    vtd_ref[...] = vt[nb:].astype(vtd_ref.dtype)


def _inproj(x2, lw, tables, bsz, seq, tm=512):
    t = x2.shape[0]
    nt = seq // tm
    widths = (W_B, W_D)
    vrows = (W_VT - GROUP_W, GROUP_W)
    hw = C_HEADS * MLA_HEAD_PAD
    cvrows = C_HEADS * MLA_V_ROWS
    tok = lambda n: pl.BlockSpec((tm, n), lambda b, i: (b * nt + i, 0))
    plane = lambda dil: pl.BlockSpec((1, dil, tm // dil, W_A), lambda b, i: (b, 0, i, 0))
    const = lambda shape: pl.BlockSpec(shape, lambda b, i: (0,) * len(shape))
    pos = pl.BlockSpec((tm, MLA_HEAD_PAD), lambda b, i: (i, 0))
    return pl.pallas_call(
        _inproj_kernel,
        grid=(bsz, nt),
        in_specs=[tok(D_MODEL), const((1, D_MODEL)), const((D_MODEL, W_A + sum(widths) + W_C)),
                  const((W_VT, D_MODEL)), pos, pos, const((1, C_Q_RANK)), const((1, C_KV_RANK)),
                  const((C_Q_RANK, 2 * hw)), const((C_KV_RANK, hw)), const((GROUP_W, C_KV_RANK))],
        out_specs=([tok(W_A), plane(4), plane(16)] + [tok(n) for n in widths]
                   + [pl.BlockSpec((r, tm), lambda b, i: (0, b * nt + i)) for r in vrows]
                   + [pl.BlockSpec((1, tm, hw), lambda b, i: (b, i, 0)),
                      pl.BlockSpec((1, tm, hw), lambda b, i: (b, i, 0)),
                      pl.BlockSpec((1, tm // MLA_KC, cvrows, MLA_KC), lambda b, i: (b, i, 0, 0))]),
        out_shape=([jax.ShapeDtypeStruct((t, W_A), BF16)]
                   + [jax.ShapeDtypeStruct((bsz, dil, seq // dil, W_A), BF16) for dil in (4, 16)]
                   + [jax.ShapeDtypeStruct((t, n), BF16) for n in widths]
                   + [jax.ShapeDtypeStruct((r, t), BF16) for r in vrows]
                   + [jax.ShapeDtypeStruct((bsz, seq, hw), BF16), jax.ShapeDtypeStruct((bsz, seq, hw), BF16),
                      jax.ShapeDtypeStruct((bsz, seq // MLA_KC, cvrows, MLA_KC), BF16)]),
        scratch_shapes=[pltpu.VMEM((W_A // LANES, tm, LANES), F32)],
        compiler_params=_cparams(2),
        name="inproj",
    )(x2, lw["norm_attn"], lw["w_in"], lw["wvt_bd"], tables[0], tables[1], lw["qn"], lw["kvn"],
      lw["wq"], lw["wk"], lw["wvt"])


def _win_attn_t_kernel(*refs, heads, ck, tq, hb, n_tiles, has_sink, has_lse, v_rows):
    q_ref, kp_ref, km_ref, kn_ref, vp_ref, vm_ref, vn_ref, bias_ref = refs[:8]
    sink_ref = refs[8] if has_sink else None
    o_ref = refs[-2] if has_lse else refs[-1]
    lse_ref = refs[-1] if has_lse else None
    ts = q_ref.shape[0]
    tkw = tq + 2 * hb
    nh = len(heads)
    kw = jnp.concatenate([kp_ref[...], km_ref[...], kn_ref[...]], axis=0)
    if v_rows:
        vw = jnp.concatenate([vp_ref[...], vm_ref[...], vn_ref[...]], axis=0).astype(F32).T.astype(BF16)
    else:
        vw = jnp.concatenate([vp_ref[...], vm_ref[...], vn_ref[...]], axis=1)
    lane = lax.broadcasted_iota(jnp.int32, (tq, ck), 1)
    masks = [(lane >= lo) & (lane < lo + HEAD_DIM) for (_, lo, _, _) in heads]
    ones = jnp.ones((MLA_V_ROWS - HEAD_DIM, tkw), vw.dtype)
    step = pl.program_id(0)
    for j in range(ts // tq):
        tile = (step * (ts // tq) + j) % n_tiles
        kind = jnp.where(tile == 0, 0, jnp.where(tile == n_tiles - 1, 2, 1))
        q = q_ref[j * tq:(j + 1) * tq, :]
        q_stack = jnp.concatenate(
            [jnp.where(mk, q[:, g * ck:(g + 1) * ck], jnp.zeros((), q.dtype))
             for (g, _, _, _), mk in zip(heads, masks)], axis=0)
        st = lax.dot_general(kw[j * tq:j * tq + tkw], q_stack, (((1,), (1,)), ((), ())),
                             preferred_element_type=F32) + bias_ref[kind]
        m = jnp.max(st, axis=0, keepdims=True)
        if has_sink:
            sink = jnp.concatenate([jnp.full((1, tq), sink_ref[i] * LOG2E, F32) for i in range(nh)], axis=1)
            m = jnp.maximum(m, sink)
            sink_p = jnp.exp2(sink - m)
        pt = jnp.exp2(st - m).astype(BF16)
        outs = [None] * nh
        lses = [None] * nh
        for i, (_, _, vrow, out_pos) in enumerate(heads):
            lhs = jnp.concatenate([vw[vrow:vrow + HEAD_DIM, j * tq:j * tq + tkw], ones], axis=0)
            r = jnp.dot(lhs, pt[:, i * tq:(i + 1) * tq], preferred_element_type=F32)
            den = r[HEAD_DIM:HEAD_DIM + 1]
            if has_sink:
                den = den + sink_p[:, i * tq:(i + 1) * tq]
            outs[out_pos] = r[:HEAD_DIM] / den
            if has_lse:
                lse = (m[:, i * tq:(i + 1) * tq] + jnp.log2(den)) * (1.0 / LOG2E)
                lses[out_pos] = jnp.broadcast_to(lse, (HEAD_DIM, tq))
        o_ref[j * tq:(j + 1) * tq, :] = jnp.concatenate(outs, axis=0).T.astype(o_ref.dtype)
        if has_lse:
            lse_ref[j * tq:(j + 1) * tq, :] = jnp.concatenate(lses, axis=0).T


def _win_attn_t(qk, vt, bias, sink, *, seq, ts, tq, hb, cq, ck, qcol, kcol, vcol, heads, has_lse, name):
    rows = qk.shape[0]
    ts = min(ts, rows)
    tq = min(tq, seq)
    per = ts // hb
    nhb = rows // hb
    nh = len(heads)
    tkw = tq + 2 * hb
    v_rows = vt is None

    def prev_idx(i):
        return jnp.maximum(i * per - 1, 0)

    def next_idx(i):
        return jnp.minimum((i + 1) * per, nhb - 1)

    in_specs = [
        pl.BlockSpec((ts, cq), lambda i: (i, qcol)),
        pl.BlockSpec((hb, ck), lambda i: (prev_idx(i), kcol)),
        pl.BlockSpec((ts, ck), lambda i: (i, kcol)),
        pl.BlockSpec((hb, ck), lambda i: (next_idx(i), kcol)),
    ]
    if v_rows:
        in_specs += [
            pl.BlockSpec((hb, GROUP_W), lambda i: (prev_idx(i), vcol)),
            pl.BlockSpec((ts, GROUP_W), lambda i: (i, vcol)),
            pl.BlockSpec((hb, GROUP_W), lambda i: (next_idx(i), vcol)),
        ]
        args = [qk] * 7
    else:
        cv = vt.shape[0]
        in_specs += [
            pl.BlockSpec((cv, hb), lambda i: (0, prev_idx(i))),
            pl.BlockSpec((cv, ts), lambda i: (0, i)),
            pl.BlockSpec((cv, hb), lambda i: (0, next_idx(i))),
        ]
        args = [qk] * 4 + [vt] * 3
    in_specs.append(pl.BlockSpec((3, tkw, nh * tq), lambda i: (0, 0, 0)))
    args.append(bias)
    if sink is not None:
        in_specs.append(pl.BlockSpec(memory_space=pltpu.SMEM))
        args.append(sink)
    out_specs = [pl.BlockSpec((ts, GROUP_W), lambda i: (i, 0))]
    out_shape = [jax.ShapeDtypeStruct((rows, GROUP_W), BF16)]
    if has_lse:
        out_specs.append(pl.BlockSpec((ts, GROUP_W), lambda i: (i, 0)))
        out_shape.append(jax.ShapeDtypeStruct((rows, GROUP_W), F32))
    return pl.pallas_call(
        functools.partial(_win_attn_t_kernel, heads=heads, ck=ck, tq=tq, hb=hb, n_tiles=seq // tq,
                          has_sink=sink is not None, has_lse=has_lse, v_rows=v_rows),
        grid=(rows // ts,),
        in_specs=in_specs,
        out_specs=out_specs,
        out_shape=out_shape,
        compiler_params=_cparams(1),
        name=name,
    )(*args)


def _transpose_bias(bias):
    return jnp.swapaxes(bias, 1, 2) * LOG2E


def _tile_indices(nt):
    return (0, min(1, nt - 1), nt - 1)


def _banded_bias(seq, tq, hb, hw, slopes, dist_scale):
    nt = seq // tq
    tabs = []
    for ti in _tile_indices(nt):
        qpos = ti * tq + np.arange(tq)[:, None]
        kpos = ti * tq - hb + np.arange(tq + 2 * hb)[None, :]
        dist = np.abs(kpos - qpos)
        valid = (dist <= hw) & (kpos >= 0) & (kpos < seq)
        pen = -np.asarray(slopes, np.float32)[:, None, None] * (dist.astype(np.float32) * np.float32(dist_scale))[None]
        tabs.append(np.where(valid[None], pen, np.float32(NEG_INF)).reshape(-1, tq + 2 * hb))
    return jnp.asarray(np.stack(tabs).astype(np.float32))


def _na_bias(rpb, seq, rows_per_tile):
    rows = seq // GRID_W
    kh = min(NA_KH, rows)
    nt = rows // rows_per_tile
    tq = rows_per_tile * GRID_W
    cq = np.arange(GRID_W)[:, None]
    ck = np.arange(GRID_W)[None, :]
    col_start = np.clip(cq - NA_KW // 2, 0, GRID_W - NA_KW)
    col_ok = (ck >= col_start) & (ck < col_start + NA_KW)
    dc = np.clip(ck - cq, -(NA_KW - 1), NA_KW - 1) + (NA_KW - 1)
    col_hot = jnp.asarray(np.eye(2 * NA_KW - 1, dtype=np.float32)[dc])
    tabs = []
    for ti in _tile_indices(nt):
        rq = (ti * rows_per_tile + np.arange(rows_per_tile))[:, None]
        rk = ((ti - 1) * rows_per_tile + np.arange(3 * rows_per_tile))[None, :]
        row_start = np.clip(rq - kh // 2, 0, rows - kh)
        row_ok = (rk >= row_start) & (rk < row_start + kh)
        dr = np.clip(rk - rq + (NA_KH - 1), 0, 2 * NA_KH - 2)
        row_hot = jnp.asarray(np.eye(2 * NA_KH - 1, dtype=np.float32)[dr])
        vals = jnp.einsum("abr,hrc,xyc->haxby", row_hot, rpb.astype(F32), col_hot,
                          precision=lax.Precision.HIGHEST)
        valid = row_ok[:, None, :, None] & col_ok[None, :, None, :]
        tabs.append(jnp.where(valid[None], vals, NEG_INF).reshape(-1, 3 * tq))
    return jnp.stack(tabs)


def _alibi_slopes():
    s = np.exp2(-8.0 * np.arange(1, 9, dtype=np.float32) / 8.0)
    return s[1::2], s[0::2]


GQA_HEAD_ORDER = (0, 2, 1, 3)
HEADS_T_FULL = tuple((0, 64 * h, 64 * h, h) for h in range(4))
HEADS_T_GQA = ((0, 0, 0, 0), (0, 64, 64, 2), (1, 0, 0, 1), (1, 64, 64, 3))

A_HALO = 64


def _mixer_a(qkv_planes, seq):
    slopes_a, _ = _alibi_slopes()
    outs = []
    for (window, dil), qkv in zip(A_CONFIGS, qkv_planes):
        length = seq // dil
        hw = window // (2 * dil)
        tq = min(128, length)
        bias = _banded_bias(length, tq, A_HALO, hw, slopes_a, float(dil))
        outs.append(_win_attn_t(
            qkv.reshape(-1, W_A), None, _transpose_bias(bias), None, seq=length, ts=1024, tq=tq, hb=A_HALO,
            cq=GROUP_W, ck=GROUP_W, qcol=0, kcol=1, vcol=2, heads=HEADS_T_FULL, has_lse=True,
            name=f"mix_a_d{dil}"))
    return outs


def _mixer_b(qk_b, vt_b, sink, seq):
    _, slopes_b = _alibi_slopes()
    tq = 256
    bias = _banded_bias(seq, tq, B_HALF_WINDOW, B_HALF_WINDOW, slopes_b[list(GQA_HEAD_ORDER)], 1.0)
    (o,) = _win_attn_t(
        qk_b, vt_b, _transpose_bias(bias), sink, seq=seq, ts=1024, tq=tq, hb=B_HALF_WINDOW,
        cq=GROUP_W, ck=128, qcol=0, kcol=2, vcol=None, heads=HEADS_T_GQA, has_lse=False, name="mix_b")
    return o


def _mixer_d(qk_d, vt_d, rpb, seq):
    rows_per_tile = 4
    tq = rows_per_tile * GRID_W
    bias = _na_bias(rpb, seq, rows_per_tile)
    (o,) = _win_attn_t(
        qk_d, vt_d, _transpose_bias(bias), None, seq=seq, ts=1024, tq=tq, hb=tq,
        cq=GROUP_W, ck=GROUP_W, qcol=0, kcol=1, vcol=None, heads=HEADS_T_FULL, has_lse=False, name="mix_d")
    return o


def _mla_flash_kernel(q_ref, k_ref, vt_ref, o_ref, m_sc, acc_sc, s_sc, *, tk, depth):
    hp, hv, kc = MLA_HEAD_PAD, MLA_V_ROWS, MLA_KC
    kv = pl.program_id(2)
    base = kv * (tk // kc)

    @pl.when(kv == 0)
    def _():
        m_sc[...] = jnp.full(m_sc.shape, NEG_INF, F32)
        acc_sc[...] = jnp.zeros(acc_sc.shape, F32)

    units = [(c, h) for c in range(tk // kc) for h in range(C_HEADS)]

    def scores(u):
        c, h = units[u]
        s_sc[u % depth] = lax.dot_general(k_ref[0, base + c, :, h * hp:(h + 1) * hp],
                                          q_ref[0, :, h * hp:(h + 1) * hp],
                                          (((1,), (1,)), ((), ())), preferred_element_type=F32)

    for u in range(min(depth, len(units))):
        scores(u)
    for u, (c, h) in enumerate(units):
        st = s_sc[u % depth]
        m_old = m_sc[h:h + 1, :]
        m_new = jnp.maximum(m_old, jnp.max(st, axis=0, keepdims=True))
        alpha = jnp.exp2(m_old - m_new)
        pt = jnp.exp2(st - m_new).astype(BF16)
        m_sc[h:h + 1, :] = m_new
        if u + depth < len(units):
            scores(u + depth)
        rows = slice(h * hv, (h + 1) * hv)
        acc_sc[rows, :] = alpha * acc_sc[rows, :] + jnp.dot(
            vt_ref[0, base + c, rows, :], pt, preferred_element_type=F32)

    @pl.when(kv == pl.num_programs(2) - 1)
    def _():
        ot = jnp.concatenate(
            [acc_sc[h * hv:h * hv + HEAD_DIM, :] / acc_sc[h * hv + HEAD_DIM:h * hv + HEAD_DIM + 1, :]
             for h in range(C_HEADS)], axis=0)
        o_ref[0] = ot.T.astype(o_ref.dtype)


def _mla_flash(q, k, vt, tq=512, tk=4096, depth=8):
    b, s, hw = q.shape
    kc = MLA_KC
    tq = min(tq, s)
    tk = min(tk, s)
    vrows = C_HEADS * MLA_V_ROWS
    resident = lambda shape: pl.BlockSpec(shape, lambda n, i, j: (n, 0, 0, 0), pipeline_mode=pl.Buffered(1))
    return pl.pallas_call(
        functools.partial(_mla_flash_kernel, tk=tk, depth=depth),
        grid=(b, s // tq, s // tk),
        in_specs=[pl.BlockSpec((1, tq, hw), lambda n, i, j: (n, i, 0)),
                  resident((1, s // kc, kc, hw)),
                  resident((1, s // kc, vrows, kc))],
        out_specs=pl.BlockSpec((1, tq, GROUP_W), lambda n, i, j: (n, i, 0)),
        out_shape=jax.ShapeDtypeStruct((b, s, GROUP_W), BF16),
        scratch_shapes=[pltpu.VMEM((8, tq), F32), pltpu.VMEM((vrows, tq), F32),
                        pltpu.VMEM((depth, kc, tq), F32)],
        compiler_params=_cparams(3),
        name="mla_flash",
    )(q, k.reshape(b, s // kc, kc, hw), vt)


def _rope_tables(seq):
    inv_freq = 1.0 / (ROPE_THETA ** (jnp.arange(0, C_ROPE, 2, dtype=F32) / C_ROPE))
    ang = jnp.arange(seq, dtype=F32)[:, None] * inv_freq[None, :]
    cos, sin = jnp.cos(ang), jnp.sin(ang)
    pad = MLA_HEAD_PAD - C_NOPE - C_ROPE
    ct = jnp.concatenate([jnp.ones((seq, C_NOPE), F32), cos, cos, jnp.zeros((seq, pad), F32)], axis=1)
    st = jnp.concatenate([jnp.zeros((seq, C_NOPE), F32), -sin, sin, jnp.zeros((seq, pad), F32)], axis=1)
    return ct, st


def _outproj_kernel(a1_ref, a2_ref, a3_ref, l1_ref, l2_ref, l3_ref, ob_ref, oc_ref, od_ref,
                    x_ref, g_ref, w_ref, gm_ref, wu_ref, wd_ref, gf_ref, o_ref, *scratch, final, ff_chunk):
    tm = x_ref.shape[0]
    ncol = GROUP_W // LANES

    def natural(ref, dil, sc):
        for r in range(dil):
            for c in range(ncol):
                sc[c, pl.ds(r, tm // dil, stride=dil), :] = ref[0, r, :, c * LANES:(c + 1) * LANES].astype(F32)
        return jnp.concatenate([sc[c] for c in range(ncol)], axis=1)

    a_s = [a1_ref[...].astype(F32), natural(a2_ref, 4, scratch[0]), natural(a3_ref, 16, scratch[1])]
    ls = [l1_ref[...], natural(l2_ref, 4, scratch[2]), natural(l3_ref, 16, scratch[3])]
    m = jnp.maximum(jnp.maximum(ls[0], ls[1]), ls[2])
    es = [jnp.exp(l - m) for l in ls]
    den = es[0] + es[1] + es[2]
    oa = sum((e / den) * a for e, a in zip(es, a_s))
    groups = [oa, ob_ref[...].astype(F32), oc_ref[...].astype(F32), od_ref[...].astype(F32)]
    y = jnp.concatenate([_rms(o, g_ref[i:i + 1, :]).astype(BF16) for i, o in enumerate(groups)], axis=1)
    x = x_ref[...] + jnp.dot(y, w_ref[...], preferred_element_type=F32)
    h = _rms(x, gm_ref[...]).astype(BF16)
    acc = x
    for c in range(D_FF // ff_chunk):
        cols = slice(c * ff_chunk, (c + 1) * ff_chunk)
        u = jnp.dot(h, wu_ref[:, cols], preferred_element_type=F32)
        u = jnp.square(jnp.maximum(u, 0.0)).astype(BF16)
        acc = acc + jnp.dot(u, wd_ref[cols, :], preferred_element_type=F32)
    if final:
        acc = _rms(acc, gf_ref[...])
    o_ref[...] = acc


def _outproj_mlp(a_outs, ob, oc, od, x2, lw, gf, final, bsz, seq, tm=512, ff_chunk=1024):
    t = x2.shape[0]
    nt = seq // tm
    tok = lambda n: pl.BlockSpec((tm, n), lambda b, i: (b * nt + i, 0))
    plane = lambda dil: pl.BlockSpec((1, dil, tm // dil, GROUP_W), lambda b, i: (b, 0, i, 0))
    dils = [dil for _, dil in A_CONFIGS]
    a_specs = [tok(GROUP_W) if dil == 1 else plane(dil) for dil in dils]
    planes = lambda arr, dil: arr if dil == 1 else arr.reshape(bsz, dil, seq // dil, GROUP_W)
    args = ([planes(o, dil) for (o, _), dil in zip(a_outs, dils)] + [planes(l, dil) for (_, l), dil in zip(a_outs, dils)]
            + [ob, oc, od, x2, lw["gn"], lw["w_out"], lw["norm_mlp"], lw["w_up"], lw["w_down"], gf])
    const = lambda shape: pl.BlockSpec(shape, lambda b, i: (0,) * len(shape))
    return pl.pallas_call(
        functools.partial(_outproj_kernel, final=final, ff_chunk=ff_chunk),
        grid=(bsz, nt),
        in_specs=a_specs + a_specs + [tok(GROUP_W)] * 3 + [
            tok(D_MODEL), const((N_GROUPS, GROUP_W)), const((D_MODEL, D_MODEL)),
            const((1, D_MODEL)), const((D_MODEL, D_FF)), const((D_FF, D_MODEL)), const((1, D_MODEL))],
        out_specs=tok(D_MODEL),
        out_shape=jax.ShapeDtypeStruct((t, D_MODEL), F32),
        scratch_shapes=[pltpu.VMEM((GROUP_W // LANES, tm, LANES), F32) for _ in range(4)],
        compiler_params=_cparams(2),
        name="outproj_mlp_final" if final else "outproj_mlp",
    )(*args)


def _gqa_perm():
    return np.concatenate([np.arange(h * HEAD_DIM, (h + 1) * HEAD_DIM) for h in GQA_HEAD_ORDER])


def _layer_weights(lp):
    (norm_attn, w_in, mla_q_norm, w_q_up, mla_kv_norm, w_kv_up, sink_logits, na_rpb,
     group_norm, w_out, norm_mlp, w_mlp_up, w_mlp_down) = lp
    perm = _gqa_perm()
    half = C_ROPE // 2
    z = lambda n: jnp.zeros((D_MODEL, n), w_in.dtype)
    kr = w_in[:, 1664:1696]
    kr_sw = jnp.concatenate([kr[:, half:], kr[:, :half]], axis=1)
    pad = MLA_HEAD_PAD - C_NOPE - C_ROPE
    qscale = HEAD_DIM ** -0.5 * LOG2E
    w_in2 = jnp.concatenate([
        w_in[:, 0:256] * qscale, w_in[:, 256:768],
        w_in[:, 768:1024][:, perm] * qscale, w_in[:, 1024:1152],
        w_in[:, 1696:1952] * qscale, w_in[:, 1952:2208],
        w_in[:, 1280:1664],
        z(C_NOPE), kr, z(pad), z(C_NOPE), kr_sw, z(pad),
    ], axis=1).astype(BF16)

    wq = w_q_up.reshape(C_Q_RANK, C_HEADS, C_NOPE + C_ROPE)
    zq = lambda n: jnp.zeros((C_Q_RANK, C_HEADS, n), w_q_up.dtype)
    rope = wq[:, :, C_NOPE:]
    rope_sw = jnp.concatenate([rope[:, :, half:], rope[:, :, :half]], axis=2)
    wq_main = jnp.concatenate([wq, zq(pad)], axis=2).reshape(C_Q_RANK, -1)
    wq_swap = jnp.concatenate([zq(C_NOPE), rope_sw, zq(pad)], axis=2).reshape(C_Q_RANK, -1)
    wq2 = jnp.concatenate([wq_main, wq_swap], axis=1).astype(BF16)

    wkv = w_kv_up.reshape(C_KV_RANK, C_HEADS, C_NOPE + HEAD_DIM)
    wk2 = jnp.concatenate([wkv[:, :, :C_NOPE], jnp.zeros((C_KV_RANK, C_HEADS, MLA_HEAD_PAD - C_NOPE), w_kv_up.dtype)],
                          axis=2).reshape(C_KV_RANK, -1).astype(BF16)
    wvt = wkv[:, :, C_NOPE:].reshape(C_KV_RANK, -1).T.astype(BF16)

    wvt_bd = jnp.concatenate([w_in[:, 1152:1280], w_in[:, 2208:2464]], axis=1).T.astype(BF16)
    return dict(
        norm_attn=norm_attn.reshape(1, -1), w_in=w_in2, wvt_bd=wvt_bd,
        qn=mla_q_norm.reshape(1, -1), kvn=mla_kv_norm.reshape(1, -1), wq=wq2, wk=wk2, wvt=wvt,
        sink=sink_logits[jnp.asarray(GQA_HEAD_ORDER)], rpb=na_rpb,
        gn=group_norm, w_out=w_out.astype(BF16), norm_mlp=norm_mlp.reshape(1, -1),
        w_up=w_mlp_up.astype(BF16), w_down=w_mlp_down.astype(BF16))


def _layer(x2, bsz, seq, lw, tables, gf, final):
    qkv_a, qkv_a4, qkv_a16, qk_b, qk_d, vt_b, vt_d, q, k, vt = _inproj(x2, lw, tables, bsz, seq)
    a_outs = _mixer_a((qkv_a, qkv_a4, qkv_a16), seq)
    ob = _mixer_b(qk_b, vt_b, lw["sink"], seq)
    oc = _mla_flash(q, k, vt).reshape(bsz * seq, GROUP_W)
    od = _mixer_d(qk_d, vt_d, lw["rpb"], seq)
    return _outproj_mlp(a_outs, ob, oc, od, x2, lw, gf, final, bsz, seq)


def _trunk(x, layer_ws, gf):
    bsz, seq, _ = x.shape
    tables = _rope_tables(seq)
    x2 = x.reshape(bsz * seq, D_MODEL)
    for i, lw in enumerate(layer_ws):
        x2 = _layer(x2, bsz, seq, lw, tables, gf, final=(i == len(layer_ws) - 1))
    return x2.reshape(bsz, seq, D_MODEL)


def kernel(x_prompt, x_sample, norm_attn, w_in, mla_q_norm, w_q_up, mla_kv_norm, w_kv_up, sink_logits, na_rpb, group_norm, w_out, norm_mlp, w_mlp_up, w_mlp_down, norm_final):
    params = (norm_attn, w_in, mla_q_norm, w_q_up, mla_kv_norm, w_kv_up, sink_logits, na_rpb,
              group_norm, w_out, norm_mlp, w_mlp_up, w_mlp_down)
    depth = norm_attn.shape[0]
    layer_ws = [_layer_weights(tuple(p[i] for p in params)) for i in range(depth)]
    gf = norm_final.reshape(1, -1)
    return (_trunk(x_prompt, layer_ws, gf), _trunk(x_sample, layer_ws, gf))
```

```python
import functools

import numpy as np
import jax
import jax.numpy as jnp
from jax import lax
from jax.experimental import pallas as pl
from jax.experimental.pallas import tpu as pltpu

F32 = jnp.float32
BF16 = jnp.bfloat16

D_MODEL = 1024
HEAD_DIM = 64
N_GROUPS = 4
GROUP_W = 256
A_CONFIGS = ((128, 1), (512, 4), (2048, 16))
B_HALF_WINDOW = 128
C_Q_RANK = 256
C_KV_RANK = 128
C_NOPE = 64
C_ROPE = 32
C_HEADS = 4
ROPE_THETA = 10000.0
GRID_W = 64
NA_KH = 8
NA_KW = 16
D_FF = 4096
EPS = 1e-5
NEG_INF = -1e30

W_A = 768
W_B = 384
W_D = 512
W_VT = 384
W_C = 640
MLA_HEAD_PAD = 128
MLA_V_ROWS = 80
LOG2E = 1.4426950408889634
LANES = 128

VMEM_LIMIT = 56 * 1024 * 1024


def _cparams(n_axes):
    return pltpu.CompilerParams(dimension_semantics=("arbitrary",) * n_axes,
                                vmem_limit_bytes=VMEM_LIMIT)


def _rms(x, g):
    return x * lax.rsqrt(jnp.mean(x * x, axis=-1, keepdims=True) + EPS) * g


def _mla_prep(c, ct, st, ctt, stt, qn, kvn, wqt_ref, wk_ref, wvt_ref, q_ref, k_ref, vt_ref):
    hp = MLA_HEAD_PAD
    cqn = _rms(c[:, :C_Q_RANK], qn).astype(BF16)
    qq = lax.dot_general(wqt_ref[...], cqn, (((1,), (1,)), ((), ())),
                         preferred_element_type=F32)
    scale = (C_NOPE + C_ROPE) ** -0.5 * LOG2E
    q_ref[0] = jnp.concatenate(
        [(qq[h * hp:(h + 1) * hp] * ctt + qq[(C_HEADS + h) * hp:(C_HEADS + h + 1) * hp] * stt) * scale
         for h in range(C_HEADS)], axis=0).astype(q_ref.dtype)
    ckvn = _rms(c[:, C_Q_RANK:C_Q_RANK + C_KV_RANK], kvn).astype(BF16)
    kk = jnp.dot(ckvn, wk_ref[...], preferred_element_type=F32)
    o0 = C_Q_RANK + C_KV_RANK
    kpe = c[:, o0:o0 + hp] * ct + c[:, o0 + hp:o0 + 2 * hp] * st
    k_ref[0] = jnp.concatenate([kk[:, h * hp:(h + 1) * hp] + kpe for h in range(C_HEADS)],
                               axis=1).astype(k_ref.dtype)
    vt = lax.dot_general(wvt_ref[...], ckvn, (((1,), (1,)), ((), ())), preferred_element_type=F32)
    ones = jnp.ones((MLA_V_ROWS - HEAD_DIM, vt.shape[1]), F32)
    vt_ref[0] = jnp.concatenate(
        [blk for h in range(C_HEADS) for blk in (vt[h * HEAD_DIM:(h + 1) * HEAD_DIM], ones)],
        axis=0).astype(vt_ref.dtype)


def _inproj_kernel(x_ref, g_ref, w_ref, wvt_ref, ct_ref, st_ref, ctt_ref, stt_ref, qn_ref, kvn_ref, wq_ref, wk_ref,
                   wcvt_ref,
                   oa_ref, oa4_ref, oa16_ref, ob_ref, od_ref, vtb_ref, vtd_ref, cq_ref, ck_ref, cvt_ref, a_sc):
    h = _rms(x_ref[...], g_ref[...]).astype(BF16)
    tm = h.shape[0]
    qkv_a = jnp.dot(h, w_ref[:, :W_A], preferred_element_type=F32)
    oa_ref[...] = qkv_a.astype(oa_ref.dtype)
    for c in range(W_A // LANES):
        a_sc[c] = qkv_a[:, c * LANES:(c + 1) * LANES]
    for dil, o_ref in ((4, oa4_ref), (16, oa16_ref)):
        for r in range(dil):
            o_ref[0, r] = jnp.concatenate(
                [a_sc[c, pl.ds(r, tm // dil, stride=dil), :] for c in range(W_A // LANES)],
                axis=1).astype(o_ref.dtype)
    off = W_A
    for o_ref in (ob_ref, od_ref):
        n = o_ref.shape[-1]
        o_ref[...] = jnp.dot(h, w_ref[:, off:off + n], preferred_element_type=F32).astype(o_ref.dtype)
        off += n
    c = jnp.dot(h, w_ref[:, off:off + W_C], preferred_element_type=F32)
    _mla_prep(c, ct_ref[...], st_ref[...], ctt_ref[...], stt_ref[...], qn_ref[...], kvn_ref[...],
              wq_ref, wk_ref, wcvt_ref, cq_ref, ck_ref, cvt_ref)
    vt = lax.dot_general(wvt_ref[...], h, (((1,), (1,)), ((), ())), preferred_element_type=F32)
    nb = vtb_ref.shape[0]
    vtb_ref[...] = vt[:nb].astype(vtb_ref.dtype)
    vtd_ref[...] = vt[nb:].astype(vtd_ref.dtype)


def _inproj(x2, lw, tables, bsz, seq, tm=512):
    t = x2.shape[0]
    nt = seq // tm
    widths = (W_B, W_D)
    vrows = (W_VT - GROUP_W, GROUP_W)
    hw = C_HEADS * MLA_HEAD_PAD
    cvrows = C_HEADS * MLA_V_ROWS
    tok = lambda n: pl.BlockSpec((tm, n), lambda b, i: (b * nt + i, 0))
    plane = lambda dil: pl.BlockSpec((1, dil, tm // dil, W_A), lambda b, i: (b, 0, i, 0))
    const = lambda shape: pl.BlockSpec(shape, lambda b, i: (0,) * len(shape))
    pos = pl.BlockSpec((tm, MLA_HEAD_PAD), lambda b, i: (i, 0))
    pos_t = pl.BlockSpec((MLA_HEAD_PAD, tm), lambda b, i: (0, i))
    return pl.pallas_call(
        _inproj_kernel,
        grid=(bsz, nt),
        in_specs=[tok(D_MODEL), const((1, D_MODEL)), const((D_MODEL, W_A + sum(widths) + W_C)),
                  const((W_VT, D_MODEL)), pos, pos, pos_t, pos_t, const((1, C_Q_RANK)), const((1, C_KV_RANK)),
                  const((2 * hw, C_Q_RANK)), const((C_KV_RANK, hw)), const((GROUP_W, C_KV_RANK))],
        out_specs=([tok(W_A), plane(4), plane(16)] + [tok(n) for n in widths]
                   + [pl.BlockSpec((r, tm), lambda b, i: (0, b * nt + i)) for r in vrows]
                   + [pl.BlockSpec((1, hw, tm), lambda b, i: (b, 0, i)),
                      pl.BlockSpec((1, tm, hw), lambda b, i: (b, i, 0)),
                      pl.BlockSpec((1, cvrows, tm), lambda b, i: (b, 0, i))]),
        out_shape=([jax.ShapeDtypeStruct((t, W_A), BF16)]
                   + [jax.ShapeDtypeStruct((bsz, dil, seq // dil, W_A), BF16) for dil in (4, 16)]
                   + [jax.ShapeDtypeStruct((t, n), BF16) for n in widths]
                   + [jax.ShapeDtypeStruct((r, t), BF16) for r in vrows]
                   + [jax.ShapeDtypeStruct((bsz, hw, seq), BF16), jax.ShapeDtypeStruct((bsz, seq, hw), BF16),
                      jax.ShapeDtypeStruct((bsz, cvrows, seq), BF16)]),
        scratch_shapes=[pltpu.VMEM((W_A // LANES, tm, LANES), F32)],
        compiler_params=_cparams(2),
        name="inproj",
    )(x2, lw["norm_attn"], lw["w_in"], lw["wvt_bd"], *tables, lw["qn"], lw["kvn"],
      lw["wq"], lw["wk"], lw["wvt"])


def _win_attn_t_kernel(*refs, heads, ck, tq, hb, n_tiles, has_sink, has_lse, v_rows):
    q_ref, kp_ref, km_ref, kn_ref, vp_ref, vm_ref, vn_ref, bias_ref = refs[:8]
    sink_ref = refs[8] if has_sink else None
    o_ref = refs[-2] if has_lse else refs[-1]
    lse_ref = refs[-1] if has_lse else None
    ts = q_ref.shape[0]
    tkw = tq + 2 * hb
    nh = len(heads)
    kw = jnp.concatenate([kp_ref[...], km_ref[...], kn_ref[...]], axis=0)
    if v_rows:
        vw = jnp.concatenate([vp_ref[...], vm_ref[...], vn_ref[...]], axis=0).astype(F32).T.astype(BF16)
    else:
        vw = jnp.concatenate([vp_ref[...], vm_ref[...], vn_ref[...]], axis=1)
    row = lax.broadcasted_iota(jnp.int32, (ck, tq), 0)
    masks = [(row >= lo) & (row < lo + HEAD_DIM) for (_, lo, _, _) in heads]
    qt_all = q_ref[...].astype(F32).T.astype(BF16)
    ones = jnp.ones((MLA_V_ROWS - HEAD_DIM, tkw), vw.dtype)
    step = pl.program_id(0)
    for j in range(ts // tq):
        tile = (step * (ts // tq) + j) % n_tiles
        kind = jnp.where(tile == 0, 0, jnp.where(tile == n_tiles - 1, 2, 1))
        qt = qt_all[:, j * tq:(j + 1) * tq]
        q_stack = jnp.concatenate(
            [jnp.where(mk, qt[g * ck:(g + 1) * ck, :], jnp.zeros((), qt.dtype))
             for (g, _, _, _), mk in zip(heads, masks)], axis=1)
        st = jnp.dot(kw[j * tq:j * tq + tkw], q_stack, preferred_element_type=F32) + bias_ref[kind]
        m = jnp.max(st, axis=0, keepdims=True)
        if has_sink:
            sink = jnp.concatenate([jnp.full((1, tq), sink_ref[i] * LOG2E, F32) for i in range(nh)], axis=1)
            m = jnp.maximum(m, sink)
            sink_p = jnp.exp2(sink - m)
        pt = jnp.exp2(st - m).astype(BF16)
        outs = [None] * nh
        lses = [None] * nh
        for i, (_, _, vrow, out_pos) in enumerate(heads):
            lhs = jnp.concatenate([vw[vrow:vrow + HEAD_DIM, j * tq:j * tq + tkw], ones], axis=0)
            r = jnp.dot(lhs, pt[:, i * tq:(i + 1) * tq], preferred_element_type=F32)
            den = r[HEAD_DIM:HEAD_DIM + 1]
            if has_sink:
                den = den + sink_p[:, i * tq:(i + 1) * tq]
            outs[out_pos] = r[:HEAD_DIM] / den
            if has_lse:
                lse = (m[:, i * tq:(i + 1) * tq] + jnp.log2(den)) * (1.0 / LOG2E)
                lses[out_pos] = jnp.broadcast_to(lse, (HEAD_DIM, tq))
        o_ref[j * tq:(j + 1) * tq, :] = jnp.concatenate(outs, axis=0).T.astype(o_ref.dtype)
        if has_lse:
            lse_ref[j * tq:(j + 1) * tq, :] = jnp.concatenate(lses, axis=0).T


def _win_attn_t(qk, vt, bias, sink, *, seq, ts, tq, hb, cq, ck, qcol, kcol, vcol, heads, has_lse, name):
    rows = qk.shape[0]
    ts = min(ts, rows)
    tq = min(tq, seq)
    per = ts // hb
    nhb = rows // hb
    nh = len(heads)
    tkw = tq + 2 * hb
    v_rows = vt is None

    def prev_idx(i):
        return jnp.maximum(i * per - 1, 0)

    def next_idx(i):
        return jnp.minimum((i + 1) * per, nhb - 1)

    in_specs = [
        pl.BlockSpec((ts, cq), lambda i: (i, qcol)),
        pl.BlockSpec((hb, ck), lambda i: (prev_idx(i), kcol)),
        pl.BlockSpec((ts, ck), lambda i: (i, kcol)),
        pl.BlockSpec((hb, ck), lambda i: (next_idx(i), kcol)),
    ]
    if v_rows:
        in_specs += [
            pl.BlockSpec((hb, GROUP_W), lambda i: (prev_idx(i), vcol)),
            pl.BlockSpec((ts, GROUP_W), lambda i: (i, vcol)),
            pl.BlockSpec((hb, GROUP_W), lambda i: (next_idx(i), vcol)),
        ]
        args = [qk] * 7
    else:
        cv = vt.shape[0]
        in_specs += [
            pl.BlockSpec((cv, hb), lambda i: (0, prev_idx(i))),
            pl.BlockSpec((cv, ts), lambda i: (0, i)),
            pl.BlockSpec((cv, hb), lambda i: (0, next_idx(i))),
        ]
        args = [qk] * 4 + [vt] * 3
    in_specs.append(pl.BlockSpec((3, tkw, nh * tq), lambda i: (0, 0, 0)))
    args.append(bias)
    if sink is not None:
        in_specs.append(pl.BlockSpec(memory_space=pltpu.SMEM))
        args.append(sink)
    out_specs = [pl.BlockSpec((ts, GROUP_W), lambda i: (i, 0))]
    out_shape = [jax.ShapeDtypeStruct((rows, GROUP_W), BF16)]
    if has_lse:
        out_specs.append(pl.BlockSpec((ts, GROUP_W), lambda i: (i, 0)))
        out_shape.append(jax.ShapeDtypeStruct((rows, GROUP_W), F32))
    return pl.pallas_call(
        functools.partial(_win_attn_t_kernel, heads=heads, ck=ck, tq=tq, hb=hb, n_tiles=seq // tq,
                          has_sink=sink is not None, has_lse=has_lse, v_rows=v_rows),
        grid=(rows // ts,),
        in_specs=in_specs,
        out_specs=out_specs,
        out_shape=out_shape,
        compiler_params=_cparams(1),
        name=name,
    )(*args)


def _transpose_bias(bias):
    return jnp.swapaxes(bias, 1, 2) * LOG2E


def _tile_indices(nt):
    return (0, min(1, nt - 1), nt - 1)


def _banded_bias(seq, tq, hb, hw, slopes, dist_scale):
    nt = seq // tq
    tabs = []
    for ti in _tile_indices(nt):
        qpos = ti * tq + np.arange(tq)[:, None]
        kpos = ti * tq - hb + np.arange(tq + 2 * hb)[None, :]
        dist = np.abs(kpos - qpos)
        valid = (dist <= hw) & (kpos >= 0) & (kpos < seq)
        pen = -np.asarray(slopes, np.float32)[:, None, None] * (dist.astype(np.float32) * np.float32(dist_scale))[None]
        tabs.append(np.where(valid[None], pen, np.float32(NEG_INF)).reshape(-1, tq + 2 * hb))
    return jnp.asarray(np.stack(tabs).astype(np.float32))


def _na_bias(rpb, seq, rows_per_tile):
    rows = seq // GRID_W
    kh = min(NA_KH, rows)
    nt = rows // rows_per_tile
    tq = rows_per_tile * GRID_W
    cq = np.arange(GRID_W)[:, None]
    ck = np.arange(GRID_W)[None, :]
    col_start = np.clip(cq - NA_KW // 2, 0, GRID_W - NA_KW)
    col_ok = (ck >= col_start) & (ck < col_start + NA_KW)
    dc = np.clip(ck - cq, -(NA_KW - 1), NA_KW - 1) + (NA_KW - 1)
    col_hot = jnp.asarray(np.eye(2 * NA_KW - 1, dtype=np.float32)[dc])
    tabs = []
    for ti in _tile_indices(nt):
        rq = (ti * rows_per_tile + np.arange(rows_per_tile))[:, None]
        rk = ((ti - 1) * rows_per_tile + np.arange(3 * rows_per_tile))[None, :]
        row_start = np.clip(rq - kh // 2, 0, rows - kh)
        row_ok = (rk >= row_start) & (rk < row_start + kh)
        dr = np.clip(rk - rq + (NA_KH - 1), 0, 2 * NA_KH - 2)
        row_hot = jnp.asarray(np.eye(2 * NA_KH - 1, dtype=np.float32)[dr])
        vals = jnp.einsum("abr,hrc,xyc->haxby", row_hot, rpb.astype(F32), col_hot,
                          precision=lax.Precision.HIGHEST)
        valid = row_ok[:, None, :, None] & col_ok[None, :, None, :]
        tabs.append(jnp.where(valid[None], vals, NEG_INF).reshape(-1, 3 * tq))
    return jnp.stack(tabs)


def _alibi_slopes():
    s = np.exp2(-8.0 * np.arange(1, 9, dtype=np.float32) / 8.0)
    return s[1::2], s[0::2]


GQA_HEAD_ORDER = (0, 2, 1, 3)
HEADS_T_FULL = tuple((0, 64 * h, 64 * h, h) for h in range(4))
HEADS_T_GQA = ((0, 0, 0, 0), (0, 64, 64, 2), (1, 0, 0, 1), (1, 64, 64, 3))

A_HALO = 64


def _mixer_a(qkv_planes, seq):
    slopes_a, _ = _alibi_slopes()
    outs = []
    for (window, dil), qkv in zip(A_CONFIGS, qkv_planes):
        length = seq // dil
        hw = window // (2 * dil)
        tq = min(128, length)
        bias = _banded_bias(length, tq, A_HALO, hw, slopes_a, float(dil))
        outs.append(_win_attn_t(
            qkv.reshape(-1, W_A), None, _transpose_bias(bias), None, seq=length, ts=1024, tq=tq, hb=A_HALO,
            cq=GROUP_W, ck=GROUP_W, qcol=0, kcol=1, vcol=2, heads=HEADS_T_FULL, has_lse=True,
            name=f"mix_a_d{dil}"))
    return outs


def _mixer_b(qk_b, vt_b, sink, seq):
    _, slopes_b = _alibi_slopes()
    tq = 256
    bias = _banded_bias(seq, tq, B_HALF_WINDOW, B_HALF_WINDOW, slopes_b[list(GQA_HEAD_ORDER)], 1.0)
    (o,) = _win_attn_t(
        qk_b, vt_b, _transpose_bias(bias), sink, seq=seq, ts=1024, tq=tq, hb=B_HALF_WINDOW,
        cq=GROUP_W, ck=128, qcol=0, kcol=2, vcol=None, heads=HEADS_T_GQA, has_lse=False, name="mix_b")
    return o


def _mixer_d(qk_d, vt_d, rpb, seq):
    rows_per_tile = 4
    tq = rows_per_tile * GRID_W
    bias = _na_bias(rpb, seq, rows_per_tile)
    (o,) = _win_attn_t(
        qk_d, vt_d, _transpose_bias(bias), None, seq=seq, ts=1024, tq=tq, hb=tq,
        cq=GROUP_W, ck=GROUP_W, qcol=0, kcol=1, vcol=None, heads=HEADS_T_FULL, has_lse=False, name="mix_d")
    return o


def _mla_flash_kernel(q_ref, k_ref, vt_ref, o_ref, m_sc, acc_sc, s_sc, *, kc, depth):
    hp, hv = MLA_HEAD_PAD, MLA_V_ROWS
    kv = pl.program_id(2)
    tk = k_ref.shape[1]

    @pl.when(kv == 0)
    def _():
        m_sc[...] = jnp.full(m_sc.shape, NEG_INF, F32)
        acc_sc[...] = jnp.zeros(acc_sc.shape, F32)

    units = [(c, h) for c in range(tk // kc) for h in range(C_HEADS)]

    def scores(u):
        c, h = units[u]
        s_sc[u % depth] = jnp.dot(k_ref[0, c * kc:(c + 1) * kc, h * hp:(h + 1) * hp],
                                  q_ref[0, h * hp:(h + 1) * hp, :], preferred_element_type=F32)

    for u in range(min(depth, len(units))):
        scores(u)
    for u, (c, h) in enumerate(units):
        st = s_sc[u % depth]
        m_old = m_sc[h:h + 1, :]
        m_new = jnp.maximum(m_old, jnp.max(st, axis=0, keepdims=True))
        alpha = jnp.exp2(m_old - m_new)
        pt = jnp.exp2(st - m_new).astype(BF16)
        m_sc[h:h + 1, :] = m_new
        if u + depth < len(units):
            scores(u + depth)
        rows = slice(h * hv, (h + 1) * hv)
        acc_sc[rows, :] = alpha * acc_sc[rows, :] + jnp.dot(
            vt_ref[0, rows, c * kc:(c + 1) * kc], pt, preferred_element_type=F32)

    @pl.when(kv == pl.num_programs(2) - 1)
    def _():
        ot = jnp.concatenate(
            [acc_sc[h * hv:h * hv + HEAD_DIM, :] / acc_sc[h * hv + HEAD_DIM:h * hv + HEAD_DIM + 1, :]
             for h in range(C_HEADS)], axis=0)
        o_ref[0] = ot.T.astype(o_ref.dtype)


def _mla_flash(q, k, vt, tq=512, tk=4096, kc=256, depth=8):
    b, hw, s = q.shape
    tq = min(tq, s)
    tk = min(tk, s)
    vrows = C_HEADS * MLA_V_ROWS
    return pl.pallas_call(
        functools.partial(_mla_flash_kernel, kc=kc, depth=depth),
        grid=(b, s // tq, s // tk),
        in_specs=[pl.BlockSpec((1, hw, tq), lambda n, i, j: (n, 0, i)),
                  pl.BlockSpec((1, tk, hw), lambda n, i, j: (n, j, 0)),
                  pl.BlockSpec((1, vrows, tk), lambda n, i, j: (n, 0, j))],
        out_specs=pl.BlockSpec((1, tq, GROUP_W), lambda n, i, j: (n, i, 0)),
        out_shape=jax.ShapeDtypeStruct((b, s, GROUP_W), BF16),
        scratch_shapes=[pltpu.VMEM((8, tq), F32), pltpu.VMEM((vrows, tq), F32),
                        pltpu.VMEM((depth, kc, tq), F32)],
        compiler_params=_cparams(3),
        name="mla_flash",
    )(q, k, vt)


def _rope_tables(seq):
    inv_freq = 1.0 / (ROPE_THETA ** (jnp.arange(0, C_ROPE, 2, dtype=F32) / C_ROPE))
    ang = jnp.arange(seq, dtype=F32)[:, None] * inv_freq[None, :]
    cos, sin = jnp.cos(ang), jnp.sin(ang)
    pad = MLA_HEAD_PAD - C_NOPE - C_ROPE
    ct = jnp.concatenate([jnp.ones((seq, C_NOPE), F32), cos, cos, jnp.zeros((seq, pad), F32)], axis=1)
    st = jnp.concatenate([jnp.zeros((seq, C_NOPE), F32), -sin, sin, jnp.zeros((seq, pad), F32)], axis=1)
    return ct, st, ct.T, st.T


def _outproj_kernel(a1_ref, a2_ref, a3_ref, l1_ref, l2_ref, l3_ref, ob_ref, oc_ref, od_ref,
                    x_ref, g_ref, w_ref, gm_ref, wu_ref, wd_ref, gf_ref, o_ref, *scratch, final, ff_chunk):
    tm = x_ref.shape[0]
    ncol = GROUP_W // LANES

    def natural(ref, dil, sc):
        for r in range(dil):
            for c in range(ncol):
                sc[c, pl.ds(r, tm // dil, stride=dil), :] = ref[0, r, :, c * LANES:(c + 1) * LANES].astype(F32)
        return jnp.concatenate([sc[c] for c in range(ncol)], axis=1)

    a_s = [a1_ref[...].astype(F32), natural(a2_ref, 4, scratch[0]), natural(a3_ref, 16, scratch[1])]
    ls = [l1_ref[...], natural(l2_ref, 4, scratch[2]), natural(l3_ref, 16, scratch[3])]
    m = jnp.maximum(jnp.maximum(ls[0], ls[1]), ls[2])
    es = [jnp.exp(l - m) for l in ls]
    den = es[0] + es[1] + es[2]
    oa = sum((e / den) * a for e, a in zip(es, a_s))
    groups = [oa, ob_ref[...].astype(F32), oc_ref[...].astype(F32), od_ref[...].astype(F32)]
    y = jnp.concatenate([_rms(o, g_ref[i:i + 1, :]).astype(BF16) for i, o in enumerate(groups)], axis=1)
    x = x_ref[...] + jnp.dot(y, w_ref[...], preferred_element_type=F32)
    h = _rms(x, gm_ref[...]).astype(BF16)
    acc = x
    for c in range(D_FF // ff_chunk):
        cols = slice(c * ff_chunk, (c + 1) * ff_chunk)
        u = jnp.dot(h, wu_ref[:, cols], preferred_element_type=F32)
        u = jnp.square(jnp.maximum(u, 0.0)).astype(BF16)
        acc = acc + jnp.dot(u, wd_ref[cols, :], preferred_element_type=F32)
    if final:
        acc = _rms(acc, gf_ref[...])
    o_ref[...] = acc


def _outproj_mlp(a_outs, ob, oc, od, x2, lw, gf, final, bsz, seq, tm=512, ff_chunk=1024):
    t = x2.shape[0]
    nt = seq // tm
    tok = lambda n: pl.BlockSpec((tm, n), lambda b, i: (b * nt + i, 0))
    plane = lambda dil: pl.BlockSpec((1, dil, tm // dil, GROUP_W), lambda b, i: (b, 0, i, 0))
    dils = [dil for _, dil in A_CONFIGS]
    a_specs = [tok(GROUP_W) if dil == 1 else plane(dil) for dil in dils]
    planes = lambda arr, dil: arr if dil == 1 else arr.reshape(bsz, dil, seq // dil, GROUP_W)
    args = ([planes(o, dil) for (o, _), dil in zip(a_outs, dils)] + [planes(l, dil) for (_, l), dil in zip(a_outs, dils)]
            + [ob, oc, od, x2, lw["gn"], lw["w_out"], lw["norm_mlp"], lw["w_up"], lw["w_down"], gf])
    const = lambda shape: pl.BlockSpec(shape, lambda b, i: (0,) * len(shape))
    return pl.pallas_call(
        functools.partial(_outproj_kernel, final=final, ff_chunk=ff_chunk),
        grid=(bsz, nt),
        in_specs=a_specs + a_specs + [tok(GROUP_W)] * 3 + [
            tok(D_MODEL), const((N_GROUPS, GROUP_W)), const((D_MODEL, D_MODEL)),
            const((1, D_MODEL)), const((D_MODEL, D_FF)), const((D_FF, D_MODEL)), const((1, D_MODEL))],
        out_specs=tok(D_MODEL),
        out_shape=jax.ShapeDtypeStruct((t, D_MODEL), F32),
        scratch_shapes=[pltpu.VMEM((GROUP_W // LANES, tm, LANES), F32) for _ in range(4)],
        compiler_params=_cparams(2),
        name="outproj_mlp_final" if final else "outproj_mlp",
    )(*args)


def _gqa_perm():
    return np.concatenate([np.arange(h * HEAD_DIM, (h + 1) * HEAD_DIM) for h in GQA_HEAD_ORDER])


def _layer_weights(lp):
    (norm_attn, w_in, mla_q_norm, w_q_up, mla_kv_norm, w_kv_up, sink_logits, na_rpb,
     group_norm, w_out, norm_mlp, w_mlp_up, w_mlp_down) = lp
    perm = _gqa_perm()
    half = C_ROPE // 2
    z = lambda n: jnp.zeros((D_MODEL, n), w_in.dtype)
    kr = w_in[:, 1664:1696]
    kr_sw = jnp.concatenate([kr[:, half:], kr[:, :half]], axis=1)
    pad = MLA_HEAD_PAD - C_NOPE - C_ROPE
    qscale = HEAD_DIM ** -0.5 * LOG2E
    w_in2 = jnp.concatenate([
        w_in[:, 0:256] * qscale, w_in[:, 256:768],
        w_in[:, 768:1024][:, perm] * qscale, w_in[:, 1024:1152],
        w_in[:, 1696:1952] * qscale, w_in[:, 1952:2208],
        w_in[:, 1280:1664],
        z(C_NOPE), kr, z(pad), z(C_NOPE), kr_sw, z(pad),
    ], axis=1).astype(BF16)

    wq = w_q_up.reshape(C_Q_RANK, C_HEADS, C_NOPE + C_ROPE)
    zq = lambda n: jnp.zeros((C_Q_RANK, C_HEADS, n), w_q_up.dtype)
    rope = wq[:, :, C_NOPE:]
    rope_sw = jnp.concatenate([rope[:, :, half:], rope[:, :, :half]], axis=2)
    wq_main = jnp.concatenate([wq, zq(pad)], axis=2).reshape(C_Q_RANK, -1)
    wq_swap = jnp.concatenate([zq(C_NOPE), rope_sw, zq(pad)], axis=2).reshape(C_Q_RANK, -1)
    wq2 = jnp.concatenate([wq_main, wq_swap], axis=1).T.astype(BF16)

    wkv = w_kv_up.reshape(C_KV_RANK, C_HEADS, C_NOPE + HEAD_DIM)
    wk2 = jnp.concatenate([wkv[:, :, :C_NOPE], jnp.zeros((C_KV_RANK, C_HEADS, MLA_HEAD_PAD - C_NOPE), w_kv_up.dtype)],
                          axis=2).reshape(C_KV_RANK, -1).astype(BF16)
    wvt = wkv[:, :, C_NOPE:].reshape(C_KV_RANK, -1).T.astype(BF16)

    wvt_bd = jnp.concatenate([w_in[:, 1152:1280], w_in[:, 2208:2464]], axis=1).T.astype(BF16)
    return dict(
        norm_attn=norm_attn.reshape(1, -1), w_in=w_in2, wvt_bd=wvt_bd,
        qn=mla_q_norm.reshape(1, -1), kvn=mla_kv_norm.reshape(1, -1), wq=wq2, wk=wk2, wvt=wvt,
        sink=sink_logits[jnp.asarray(GQA_HEAD_ORDER)], rpb=na_rpb,
        gn=group_norm, w_out=w_out.astype(BF16), norm_mlp=norm_mlp.reshape(1, -1),
        w_up=w_mlp_up.astype(BF16), w_down=w_mlp_down.astype(BF16))


def _layer(x2, bsz, seq, lw, tables, gf, final):
    qkv_a, qkv_a4, qkv_a16, qk_b, qk_d, vt_b, vt_d, q, k, vt = _inproj(x2, lw, tables, bsz, seq)
    a_outs = _mixer_a((qkv_a, qkv_a4, qkv_a16), seq)
    ob = _mixer_b(qk_b, vt_b, lw["sink"], seq)
    oc = _mla_flash(q, k, vt).reshape(bsz * seq, GROUP_W)
    od = _mixer_d(qk_d, vt_d, lw["rpb"], seq)
    return _outproj_mlp(a_outs, ob, oc, od, x2, lw, gf, final, bsz, seq)


def _trunk(x, layer_ws, gf):
    bsz, seq, _ = x.shape
    tables = _rope_tables(seq)
    x2 = x.reshape(bsz * seq, D_MODEL)
    for i, lw in enumerate(layer_ws):
        x2 = _layer(x2, bsz, seq, lw, tables, gf, final=(i == len(layer_ws) - 1))
    return x2.reshape(bsz, seq, D_MODEL)


def kernel(x_prompt, x_sample, norm_attn, w_in, mla_q_norm, w_q_up, mla_kv_norm, w_kv_up, sink_logits, na_rpb, group_norm, w_out, norm_mlp, w_mlp_up, w_mlp_down, norm_final):
    params = (norm_attn, w_in, mla_q_norm, w_q_up, mla_kv_norm, w_kv_up, sink_logits, na_rpb,
              group_norm, w_out, norm_mlp, w_mlp_up, w_mlp_down)
    depth = norm_attn.shape[0]
    layer_ws = [_layer_weights(tuple(p[i] for p in params)) for i in range(depth)]
    gf = norm_final.reshape(1, -1)
    return (_trunk(x_prompt, layer_ws, gf), _trunk(x_sample, layer_ws, gf))
```

```python
import functools

import numpy as np
import jax
import jax.numpy as jnp
from jax import lax
from jax.experimental import pallas as pl
from jax.experimental.pallas import tpu as pltpu

F32 = jnp.float32
BF16 = jnp.bfloat16

D_MODEL = 1024
HEAD_DIM = 64
N_GROUPS = 4
GROUP_W = 256
A_CONFIGS = ((128, 1), (512, 4), (2048, 16))
B_HALF_WINDOW = 128
C_Q_RANK = 256
C_KV_RANK = 128
C_NOPE = 64
C_ROPE = 32
C_HEADS = 4
ROPE_THETA = 10000.0
GRID_W = 64
NA_KH = 8
NA_KW = 16
D_FF = 4096
EPS = 1e-5
NEG_INF = -1e30

W_A = 768
W_B = 384
W_D = 512
W_VT = 384
W_C = 640
MLA_HEAD_PAD = 128
MLA_V_ROWS = 80
LOG2E = 1.4426950408889634
LANES = 128

VMEM_LIMIT = 56 * 1024 * 1024


def _cparams(n_axes):
    return pltpu.CompilerParams(dimension_semantics=("arbitrary",) * n_axes,
                                vmem_limit_bytes=VMEM_LIMIT)


def _rms(x, g):
    return x * lax.rsqrt(jnp.mean(x * x, axis=-1, keepdims=True) + EPS) * g


def _mla_prep(c, ct, st, ctt, stt, qn, kvn, wqt_ref, wk_ref, wvt_ref, q_ref, k_ref, vt_ref):
    hp = MLA_HEAD_PAD
    cqn = _rms(c[:, :C_Q_RANK], qn).astype(BF16)
    qq = lax.dot_general(wqt_ref[...], cqn, (((1,), (1,)), ((), ())),
                         preferred_element_type=F32)
    scale = (C_NOPE + C_ROPE) ** -0.5 * LOG2E
    q_ref[0] = jnp.concatenate(
        [(qq[h * hp:(h + 1) * hp] * ctt + qq[(C_HEADS + h) * hp:(C_HEADS + h + 1) * hp] * stt) * scale
         for h in range(C_HEADS)], axis=0).astype(q_ref.dtype)
    ckvn = _rms(c[:, C_Q_RANK:C_Q_RANK + C_KV_RANK], kvn).astype(BF16)
    kk = jnp.dot(ckvn, wk_ref[...], preferred_element_type=F32)
    o0 = C_Q_RANK + C_KV_RANK
    kpe = c[:, o0:o0 + hp] * ct + c[:, o0 + hp:o0 + 2 * hp] * st
    k_ref[0] = jnp.concatenate([kk[:, h * hp:(h + 1) * hp] + kpe for h in range(C_HEADS)],
                               axis=1).astype(k_ref.dtype)
    vt = lax.dot_general(wvt_ref[...], ckvn, (((1,), (1,)), ((), ())), preferred_element_type=F32)
    ones = jnp.ones((MLA_V_ROWS - HEAD_DIM, vt.shape[1]), F32)
    vt_ref[0] = jnp.concatenate(
        [blk for h in range(C_HEADS) for blk in (vt[h * HEAD_DIM:(h + 1) * HEAD_DIM], ones)],
        axis=0).astype(vt_ref.dtype)


def _inproj_kernel(x_ref, g_ref, w_ref, wvt_ref, ct_ref, st_ref, ctt_ref, stt_ref, qn_ref, kvn_ref, wq_ref, wk_ref,
                   wcvt_ref,
                   oa_ref, oa4_ref, oa16_ref, ob_ref, od_ref, vtb_ref, vtd_ref, cq_ref, ck_ref, cvt_ref, a_sc):
    h = _rms(x_ref[...], g_ref[...]).astype(BF16)
    tm = h.shape[0]
    qkv_a = jnp.dot(h, w_ref[:, :W_A], preferred_element_type=F32)
    oa_ref[...] = qkv_a.astype(oa_ref.dtype)
    for c in range(W_A // LANES):
        a_sc[c] = qkv_a[:, c * LANES:(c + 1) * LANES]
    for dil, o_ref in ((4, oa4_ref), (16, oa16_ref)):
        for r in range(dil):
            o_ref[0, r] = jnp.concatenate(
                [a_sc[c, pl.ds(r, tm // dil, stride=dil), :] for c in range(W_A // LANES)],
                axis=1).astype(o_ref.dtype)
    off = W_A
    for o_ref in (ob_ref, od_ref):
        n = o_ref.shape[-1]
        o_ref[...] = jnp.dot(h, w_ref[:, off:off + n], preferred_element_type=F32).astype(o_ref.dtype)
        off += n
    c = jnp.dot(h, w_ref[:, off:off + W_C], preferred_element_type=F32)
    _mla_prep(c, ct_ref[...], st_ref[...], ctt_ref[...], stt_ref[...], qn_ref[...], kvn_ref[...],
              wq_ref, wk_ref, wcvt_ref, cq_ref, ck_ref, cvt_ref)
    vt = lax.dot_general(wvt_ref[...], h, (((1,), (1,)), ((), ())), preferred_element_type=F32)
    nb = vtb_ref.shape[0]
    vtb_ref[...] = vt[:nb].astype(vtb_ref.dtype)
    vtd_ref[...] = vt[nb:].astype(vtd_ref.dtype)


def _inproj(x2, lw, tables, bsz, seq, tm=512):
    t = x2.shape[0]
    nt = seq // tm
    widths = (W_B, W_D)
    vrows = (W_VT - GROUP_W, GROUP_W)
    hw = C_HEADS * MLA_HEAD_PAD
    cvrows = C_HEADS * MLA_V_ROWS
    tok = lambda n: pl.BlockSpec((tm, n), lambda b, i: (b * nt + i, 0))
    plane = lambda dil: pl.BlockSpec((1, dil, tm // dil, W_A), lambda b, i: (b, 0, i, 0))
    const = lambda shape: pl.BlockSpec(shape, lambda b, i: (0,) * len(shape))
    pos = pl.BlockSpec((tm, MLA_HEAD_PAD), lambda b, i: (i, 0))
    pos_t = pl.BlockSpec((MLA_HEAD_PAD, tm), lambda b, i: (0, i))
    return pl.pallas_call(
        _inproj_kernel,
        grid=(bsz, nt),
        in_specs=[tok(D_MODEL), const((1, D_MODEL)), const((D_MODEL, W_A + sum(widths) + W_C)),
                  const((W_VT, D_MODEL)), pos, pos, pos_t, pos_t, const((1, C_Q_RANK)), const((1, C_KV_RANK)),
                  const((2 * hw, C_Q_RANK)), const((C_KV_RANK, hw)), const((GROUP_W, C_KV_RANK))],
        out_specs=([tok(W_A), plane(4), plane(16)] + [tok(n) for n in widths]
                   + [pl.BlockSpec((r, tm), lambda b, i: (0, b * nt + i)) for r in vrows]
                   + [pl.BlockSpec((1, hw, tm), lambda b, i: (b, 0, i)),
                      pl.BlockSpec((1, tm, hw), lambda b, i: (b, i, 0)),
                      pl.BlockSpec((1, cvrows, tm), lambda b, i: (b, 0, i))]),
        out_shape=([jax.ShapeDtypeStruct((t, W_A), BF16)]
                   + [jax.ShapeDtypeStruct((bsz, dil, seq // dil, W_A), BF16) for dil in (4, 16)]
                   + [jax.ShapeDtypeStruct((t, n), BF16) for n in widths]
                   + [jax.ShapeDtypeStruct((r, t), BF16) for r in vrows]
                   + [jax.ShapeDtypeStruct((bsz, hw, seq), BF16), jax.ShapeDtypeStruct((bsz, seq, hw), BF16),
                      jax.ShapeDtypeStruct((bsz, cvrows, seq), BF16)]),
        scratch_shapes=[pltpu.VMEM((W_A // LANES, tm, LANES), F32)],
        compiler_params=_cparams(2),
        name="inproj",
    )(x2, lw["norm_attn"], lw["w_in"], lw["wvt_bd"], *tables, lw["qn"], lw["kvn"],
      lw["wq"], lw["wk"], lw["wvt"])


def _win_attn_t_kernel(*refs, heads, ck, tq, hb, n_tiles, has_sink, has_lse, v_rows):
    q_ref, kp_ref, km_ref, kn_ref, vp_ref, vm_ref, vn_ref, bias_ref = refs[:8]
    sink_ref = refs[8] if has_sink else None
    o_ref = refs[-2] if has_lse else refs[-1]
    lse_ref = refs[-1] if has_lse else None
    ts = q_ref.shape[0]
    tkw = tq + 2 * hb
    nh = len(heads)
    kw = jnp.concatenate([kp_ref[...], km_ref[...], kn_ref[...]], axis=0)
    if v_rows:
        vw = jnp.concatenate([vp_ref[...], vm_ref[...], vn_ref[...]], axis=0).astype(F32).T.astype(BF16)
    else:
        vw = jnp.concatenate([vp_ref[...], vm_ref[...], vn_ref[...]], axis=1)
    row = lax.broadcasted_iota(jnp.int32, (ck, tq), 0)
    masks = [(row >= lo) & (row < lo + HEAD_DIM) for (_, lo, _, _) in heads]
    qt_all = q_ref[...].astype(F32).T.astype(BF16)
    ones = jnp.ones((MLA_V_ROWS - HEAD_DIM, tkw), vw.dtype)
    step = pl.program_id(0)
    for j in range(ts // tq):
        tile = (step * (ts // tq) + j) % n_tiles
        kind = jnp.where(tile == 0, 0, jnp.where(tile == n_tiles - 1, 2, 1))
        qt = qt_all[:, j * tq:(j + 1) * tq]
        q_stack = jnp.concatenate(
            [jnp.where(mk, qt[g * ck:(g + 1) * ck, :], jnp.zeros((), qt.dtype))
             for (g, _, _, _), mk in zip(heads, masks)], axis=1)
        st = jnp.dot(kw[j * tq:j * tq + tkw], q_stack, preferred_element_type=F32) + bias_ref[kind]
        m = jnp.max(st, axis=0, keepdims=True)
        if has_sink:
            sink = jnp.concatenate([jnp.full((1, tq), sink_ref[i] * LOG2E, F32) for i in range(nh)], axis=1)
            m = jnp.maximum(m, sink)
            sink_p = jnp.exp2(sink - m)
        pt = jnp.exp2(st - m).astype(BF16)
        outs = [None] * nh
        lses = [None] * nh
        for i, (_, _, vrow, out_pos) in enumerate(heads):
            lhs = jnp.concatenate([vw[vrow:vrow + HEAD_DIM, j * tq:j * tq + tkw], ones], axis=0)
            r = jnp.dot(lhs, pt[:, i * tq:(i + 1) * tq], preferred_element_type=F32)
            den = r[HEAD_DIM:HEAD_DIM + 1]
            if has_sink:
                den = den + sink_p[:, i * tq:(i + 1) * tq]
            outs[out_pos] = r[:HEAD_DIM] / den
            if has_lse:
                lse = (m[:, i * tq:(i + 1) * tq] + jnp.log2(den)) * (1.0 / LOG2E)
                lses[out_pos] = jnp.broadcast_to(lse, (HEAD_DIM, tq))
        o_ref[j * tq:(j + 1) * tq, :] = jnp.concatenate(outs, axis=0).T.astype(o_ref.dtype)
        if has_lse:
            lse_ref[j * tq:(j + 1) * tq, :] = jnp.concatenate(lses, axis=0).T


def _win_attn_t(qk, vt, bias, sink, *, seq, ts, tq, hb, cq, ck, qcol, kcol, vcol, heads, has_lse, name):
    rows = qk.shape[0]
    ts = min(ts, rows)
    tq = min(tq, seq)
    per = ts // hb
    nhb = rows // hb
    nh = len(heads)
    tkw = tq + 2 * hb
    v_rows = vt is None

    def prev_idx(i):
        return jnp.maximum(i * per - 1, 0)

    def next_idx(i):
        return jnp.minimum((i + 1) * per, nhb - 1)

    in_specs = [
        pl.BlockSpec((ts, cq), lambda i: (i, qcol)),
        pl.BlockSpec((hb, ck), lambda i: (prev_idx(i), kcol)),
        pl.BlockSpec((ts, ck), lambda i: (i, kcol)),
        pl.BlockSpec((hb, ck), lambda i: (next_idx(i), kcol)),
    ]
    if v_rows:
        in_specs += [
            pl.BlockSpec((hb, GROUP_W), lambda i: (prev_idx(i), vcol)),
            pl.BlockSpec((ts, GROUP_W), lambda i: (i, vcol)),
            pl.BlockSpec((hb, GROUP_W), lambda i: (next_idx(i), vcol)),
        ]
        args = [qk] * 7
    else:
        cv = vt.shape[0]
        in_specs += [
            pl.BlockSpec((cv, hb), lambda i: (0, prev_idx(i))),
            pl.BlockSpec((cv, ts), lambda i: (0, i)),
            pl.BlockSpec((cv, hb), lambda i: (0, next_idx(i))),
        ]
        args = [qk] * 4 + [vt] * 3
    in_specs.append(pl.BlockSpec((3, tkw, nh * tq), lambda i: (0, 0, 0)))
    args.append(bias)
    if sink is not None:
        in_specs.append(pl.BlockSpec(memory_space=pltpu.SMEM))
        args.append(sink)
    out_specs = [pl.BlockSpec((ts, GROUP_W), lambda i: (i, 0))]
    out_shape = [jax.ShapeDtypeStruct((rows, GROUP_W), BF16)]
    if has_lse:
        out_specs.append(pl.BlockSpec((ts, GROUP_W), lambda i: (i, 0)))
        out_shape.append(jax.ShapeDtypeStruct((rows, GROUP_W), F32))
    return pl.pallas_call(
        functools.partial(_win_attn_t_kernel, heads=heads, ck=ck, tq=tq, hb=hb, n_tiles=seq // tq,
                          has_sink=sink is not None, has_lse=has_lse, v_rows=v_rows),
        grid=(rows // ts,),
        in_specs=in_specs,
        out_specs=out_specs,
        out_shape=out_shape,
        compiler_params=_cparams(1),
        name=name,
    )(*args)


def _transpose_bias(bias):
    return jnp.swapaxes(bias, 1, 2) * LOG2E


def _tile_indices(nt):
    return (0, min(1, nt - 1), nt - 1)


def _banded_bias(seq, tq, hb, hw, slopes, dist_scale):
    nt = seq // tq
    tabs = []
    for ti in _tile_indices(nt):
        qpos = ti * tq + np.arange(tq)[:, None]
        kpos = ti * tq - hb + np.arange(tq + 2 * hb)[None, :]
        dist = np.abs(kpos - qpos)
        valid = (dist <= hw) & (kpos >= 0) & (kpos < seq)
        pen = -np.asarray(slopes, np.float32)[:, None, None] * (dist.astype(np.float32) * np.float32(dist_scale))[None]
        tabs.append(np.where(valid[None], pen, np.float32(NEG_INF)).reshape(-1, tq + 2 * hb))
    return jnp.asarray(np.stack(tabs).astype(np.float32))


def _na_bias(rpb, seq, rows_per_tile):
    rows = seq // GRID_W
    kh = min(NA_KH, rows)
    nt = rows // rows_per_tile
    tq = rows_per_tile * GRID_W
    cq = np.arange(GRID_W)[:, None]
    ck = np.arange(GRID_W)[None, :]
    col_start = np.clip(cq - NA_KW // 2, 0, GRID_W - NA_KW)
    col_ok = (ck >= col_start) & (ck < col_start + NA_KW)
    dc = np.clip(ck - cq, -(NA_KW - 1), NA_KW - 1) + (NA_KW - 1)
    col_hot = jnp.asarray(np.eye(2 * NA_KW - 1, dtype=np.float32)[dc])
    tabs = []
    for ti in _tile_indices(nt):
        rq = (ti * rows_per_tile + np.arange(rows_per_tile))[:, None]
        rk = ((ti - 1) * rows_per_tile + np.arange(3 * rows_per_tile))[None, :]
        row_start = np.clip(rq - kh // 2, 0, rows - kh)
        row_ok = (rk >= row_start) & (rk < row_start + kh)
        dr = np.clip(rk - rq + (NA_KH - 1), 0, 2 * NA_KH - 2)
        row_hot = jnp.asarray(np.eye(2 * NA_KH - 1, dtype=np.float32)[dr])
        vals = jnp.einsum("abr,hrc,xyc->haxby", row_hot, rpb.astype(F32), col_hot,
                          precision=lax.Precision.HIGHEST)
        valid = row_ok[:, None, :, None] & col_ok[None, :, None, :]
        tabs.append(jnp.where(valid[None], vals, NEG_INF).reshape(-1, 3 * tq))
    return jnp.stack(tabs)


def _alibi_slopes():
    s = np.exp2(-8.0 * np.arange(1, 9, dtype=np.float32) / 8.0)
    return s[1::2], s[0::2]


GQA_HEAD_ORDER = (0, 2, 1, 3)
HEADS_T_FULL = tuple((0, 64 * h, 64 * h, h) for h in range(4))
HEADS_T_GQA = ((0, 0, 0, 0), (0, 64, 64, 2), (1, 0, 0, 1), (1, 64, 64, 3))

A_HALO = 64


def _mixer_a(qkv_planes, seq):
    slopes_a, _ = _alibi_slopes()
    outs = []
    for (window, dil), qkv in zip(A_CONFIGS, qkv_planes):
        length = seq // dil
        hw = window // (2 * dil)
        tq = min(128, length)
        bias = _banded_bias(length, tq, A_HALO, hw, slopes_a, float(dil))
        outs.append(_win_attn_t(
            qkv.reshape(-1, W_A), None, _transpose_bias(bias), None, seq=length, ts=2048, tq=tq, hb=A_HALO,
            cq=GROUP_W, ck=GROUP_W, qcol=0, kcol=1, vcol=2, heads=HEADS_T_FULL, has_lse=True,
            name=f"mix_a_d{dil}"))
    return outs


def _mixer_b(qk_b, vt_b, sink, seq):
    _, slopes_b = _alibi_slopes()
    tq = 256
    bias = _banded_bias(seq, tq, B_HALF_WINDOW, B_HALF_WINDOW, slopes_b[list(GQA_HEAD_ORDER)], 1.0)
    (o,) = _win_attn_t(
        qk_b, vt_b, _transpose_bias(bias), sink, seq=seq, ts=2048, tq=tq, hb=B_HALF_WINDOW,
        cq=GROUP_W, ck=128, qcol=0, kcol=2, vcol=None, heads=HEADS_T_GQA, has_lse=False, name="mix_b")
    return o


def _mixer_d(qk_d, vt_d, rpb, seq):
    rows_per_tile = 4
    tq = rows_per_tile * GRID_W
    bias = _na_bias(rpb, seq, rows_per_tile)
    (o,) = _win_attn_t(
        qk_d, vt_d, _transpose_bias(bias), None, seq=seq, ts=2048, tq=tq, hb=tq,
        cq=GROUP_W, ck=GROUP_W, qcol=0, kcol=1, vcol=None, heads=HEADS_T_FULL, has_lse=False, name="mix_d")
    return o


def _mla_flash_kernel(q_ref, k_ref, vt_ref, o_ref, m_sc, acc_sc, s_sc, *, kc, depth):
    hp, hv = MLA_HEAD_PAD, MLA_V_ROWS
    kv = pl.program_id(2)
    tk = k_ref.shape[1]

    @pl.when(kv == 0)
    def _():
        m_sc[...] = jnp.full(m_sc.shape, NEG_INF, F32)
        acc_sc[...] = jnp.zeros(acc_sc.shape, F32)

    units = [(c, h) for c in range(tk // kc) for h in range(C_HEADS)]

    def scores(u):
        c, h = units[u]
        s_sc[u % depth] = jnp.dot(k_ref[0, c * kc:(c + 1) * kc, h * hp:(h + 1) * hp],
                                  q_ref[0, h * hp:(h + 1) * hp, :], preferred_element_type=F32)

    for u in range(min(depth, len(units))):
        scores(u)
    for u, (c, h) in enumerate(units):
        st = s_sc[u % depth]
        m_old = m_sc[h:h + 1, :]
        m_new = jnp.maximum(m_old, jnp.max(st, axis=0, keepdims=True))
        alpha = jnp.exp2(m_old - m_new)
        pt = jnp.exp2(st - m_new).astype(BF16)
        m_sc[h:h + 1, :] = m_new
        if u + depth < len(units):
            scores(u + depth)
        rows = slice(h * hv, (h + 1) * hv)
        acc_sc[rows, :] = alpha * acc_sc[rows, :] + jnp.dot(
            vt_ref[0, rows, c * kc:(c + 1) * kc], pt, preferred_element_type=F32)

    @pl.when(kv == pl.num_programs(2) - 1)
    def _():
        ot = jnp.concatenate(
            [acc_sc[h * hv:h * hv + HEAD_DIM, :] / acc_sc[h * hv + HEAD_DIM:h * hv + HEAD_DIM + 1, :]
             for h in range(C_HEADS)], axis=0)
        o_ref[0] = ot.T.astype(o_ref.dtype)


def _mla_flash(q, k, vt, tq=512, tk=4096, kc=256, depth=8):
    b, hw, s = q.shape
    tq = min(tq, s)
    tk = min(tk, s)
    vrows = C_HEADS * MLA_V_ROWS
    return pl.pallas_call(
        functools.partial(_mla_flash_kernel, kc=kc, depth=depth),
        grid=(b, s // tq, s // tk),
        in_specs=[pl.BlockSpec((1, hw, tq), lambda n, i, j: (n, 0, i)),
                  pl.BlockSpec((1, tk, hw), lambda n, i, j: (n, j, 0)),
                  pl.BlockSpec((1, vrows, tk), lambda n, i, j: (n, 0, j))],
        out_specs=pl.BlockSpec((1, tq, GROUP_W), lambda n, i, j: (n, i, 0)),
        out_shape=jax.ShapeDtypeStruct((b, s, GROUP_W), BF16),
        scratch_shapes=[pltpu.VMEM((8, tq), F32), pltpu.VMEM((vrows, tq), F32),
                        pltpu.VMEM((depth, kc, tq), F32)],
        compiler_params=_cparams(3),
        name="mla_flash",
    )(q, k, vt)


def _rope_tables(seq):
    inv_freq = 1.0 / (ROPE_THETA ** (jnp.arange(0, C_ROPE, 2, dtype=F32) / C_ROPE))
    ang = jnp.arange(seq, dtype=F32)[:, None] * inv_freq[None, :]
    cos, sin = jnp.cos(ang), jnp.sin(ang)
    pad = MLA_HEAD_PAD - C_NOPE - C_ROPE
    ct = jnp.concatenate([jnp.ones((seq, C_NOPE), F32), cos, cos, jnp.zeros((seq, pad), F32)], axis=1)
    st = jnp.concatenate([jnp.zeros((seq, C_NOPE), F32), -sin, sin, jnp.zeros((seq, pad), F32)], axis=1)
    return ct, st, ct.T, st.T


def _outproj_kernel(a1_ref, a2_ref, a3_ref, l1_ref, l2_ref, l3_ref, ob_ref, oc_ref, od_ref,
                    x_ref, g_ref, w_ref, gm_ref, wu_ref, wd_ref, gf_ref, o_ref, *scratch, final, ff_chunk):
    tm = x_ref.shape[0]
    ncol = GROUP_W // LANES

    def natural(ref, dil, sc):
        for r in range(dil):
            for c in range(ncol):
                sc[c, pl.ds(r, tm // dil, stride=dil), :] = ref[0, r, :, c * LANES:(c + 1) * LANES].astype(F32)
        return jnp.concatenate([sc[c] for c in range(ncol)], axis=1)

    a_s = [a1_ref[...].astype(F32), natural(a2_ref, 4, scratch[0]), natural(a3_ref, 16, scratch[1])]
    ls = [l1_ref[...], natural(l2_ref, 4, scratch[2]), natural(l3_ref, 16, scratch[3])]
    m = jnp.maximum(jnp.maximum(ls[0], ls[1]), ls[2])
    es = [jnp.exp(l - m) for l in ls]
    den = es[0] + es[1] + es[2]
    oa = sum((e / den) * a for e, a in zip(es, a_s))
    groups = [oa, ob_ref[...].astype(F32), oc_ref[...].astype(F32), od_ref[...].astype(F32)]
    y = jnp.concatenate([_rms(o, g_ref[i:i + 1, :]).astype(BF16) for i, o in enumerate(groups)], axis=1)
    x = x_ref[...] + jnp.dot(y, w_ref[...], preferred_element_type=F32)
    h = _rms(x, gm_ref[...]).astype(BF16)
    acc = x
    for c in range(D_FF // ff_chunk):
        cols = slice(c * ff_chunk, (c + 1) * ff_chunk)
        u = jnp.dot(h, wu_ref[:, cols], preferred_element_type=F32)
        u = jnp.square(jnp.maximum(u, 0.0)).astype(BF16)
        acc = acc + jnp.dot(u, wd_ref[cols, :], preferred_element_type=F32)
    if final:
        acc = _rms(acc, gf_ref[...])
    o_ref[...] = acc


def _outproj_mlp(a_outs, ob, oc, od, x2, lw, gf, final, bsz, seq, tm=512, ff_chunk=1024):
    t = x2.shape[0]
    nt = seq // tm
    tok = lambda n: pl.BlockSpec((tm, n), lambda b, i: (b * nt + i, 0))
    plane = lambda dil: pl.BlockSpec((1, dil, tm // dil, GROUP_W), lambda b, i: (b, 0, i, 0))
    dils = [dil for _, dil in A_CONFIGS]
    a_specs = [tok(GROUP_W) if dil == 1 else plane(dil) for dil in dils]
    planes = lambda arr, dil: arr if dil == 1 else arr.reshape(bsz, dil, seq // dil, GROUP_W)
    args = ([planes(o, dil) for (o, _), dil in zip(a_outs, dils)] + [planes(l, dil) for (_, l), dil in zip(a_outs, dils)]
            + [ob, oc, od, x2, lw["gn"], lw["w_out"], lw["norm_mlp"], lw["w_up"], lw["w_down"], gf])
    const = lambda shape: pl.BlockSpec(shape, lambda b, i: (0,) * len(shape))
    return pl.pallas_call(
        functools.partial(_outproj_kernel, final=final, ff_chunk=ff_chunk),
        grid=(bsz, nt),
        in_specs=a_specs + a_specs + [tok(GROUP_W)] * 3 + [
            tok(D_MODEL), const((N_GROUPS, GROUP_W)), const((D_MODEL, D_MODEL)),
            const((1, D_MODEL)), const((D_MODEL, D_FF)), const((D_FF, D_MODEL)), const((1, D_MODEL))],
        out_specs=tok(D_MODEL),
        out_shape=jax.ShapeDtypeStruct((t, D_MODEL), F32),
        scratch_shapes=[pltpu.VMEM((GROUP_W // LANES, tm, LANES), F32) for _ in range(4)],
        compiler_params=_cparams(2),
        name="outproj_mlp_final" if final else "outproj_mlp",
    )(*args)


def _gqa_perm():
    return np.concatenate([np.arange(h * HEAD_DIM, (h + 1) * HEAD_DIM) for h in GQA_HEAD_ORDER])


def _layer_weights(lp):
    (norm_attn, w_in, mla_q_norm, w_q_up, mla_kv_norm, w_kv_up, sink_logits, na_rpb,
     group_norm, w_out, norm_mlp, w_mlp_up, w_mlp_down) = lp
    perm = _gqa_perm()
    half = C_ROPE // 2
    z = lambda n: jnp.zeros((D_MODEL, n), w_in.dtype)
    kr = w_in[:, 1664:1696]
    kr_sw = jnp.concatenate([kr[:, half:], kr[:, :half]], axis=1)
    pad = MLA_HEAD_PAD - C_NOPE - C_ROPE
    qscale = HEAD_DIM ** -0.5 * LOG2E
    w_in2 = jnp.concatenate([
        w_in[:, 0:256] * qscale, w_in[:, 256:768],
        w_in[:, 768:1024][:, perm] * qscale, w_in[:, 1024:1152],
        w_in[:, 1696:1952] * qscale, w_in[:, 1952:2208],
        w_in[:, 1280:1664],
        z(C_NOPE), kr, z(pad), z(C_NOPE), kr_sw, z(pad),
    ], axis=1).astype(BF16)

    wq = w_q_up.reshape(C_Q_RANK, C_HEADS, C_NOPE + C_ROPE)
    zq = lambda n: jnp.zeros((C_Q_RANK, C_HEADS, n), w_q_up.dtype)
    rope = wq[:, :, C_NOPE:]
    rope_sw = jnp.concatenate([rope[:, :, half:], rope[:, :, :half]], axis=2)
    wq_main = jnp.concatenate([wq, zq(pad)], axis=2).reshape(C_Q_RANK, -1)
    wq_swap = jnp.concatenate([zq(C_NOPE), rope_sw, zq(pad)], axis=2).reshape(C_Q_RANK, -1)
    wq2 = jnp.concatenate([wq_main, wq_swap], axis=1).T.astype(BF16)

    wkv = w_kv_up.reshape(C_KV_RANK, C_HEADS, C_NOPE + HEAD_DIM)
    wk2 = jnp.concatenate([wkv[:, :, :C_NOPE], jnp.zeros((C_KV_RANK, C_HEADS, MLA_HEAD_PAD - C_NOPE), w_kv_up.dtype)],
                          axis=2).reshape(C_KV_RANK, -1).astype(BF16)
    wvt = wkv[:, :, C_NOPE:].reshape(C_KV_RANK, -1).T.astype(BF16)

    wvt_bd = jnp.concatenate([w_in[:, 1152:1280], w_in[:, 2208:2464]], axis=1).T.astype(BF16)
    return dict(
        norm_attn=norm_attn.reshape(1, -1), w_in=w_in2, wvt_bd=wvt_bd,
        qn=mla_q_norm.reshape(1, -1), kvn=mla_kv_norm.reshape(1, -1), wq=wq2, wk=wk2, wvt=wvt,
        sink=sink_logits[jnp.asarray(GQA_HEAD_ORDER)], rpb=na_rpb,
        gn=group_norm, w_out=w_out.astype(BF16), norm_mlp=norm_mlp.reshape(1, -1),
        w_up=w_mlp_up.astype(BF16), w_down=w_mlp_down.astype(BF16))


def _layer(x2, bsz, seq, lw, tables, gf, final):
    qkv_a, qkv_a4, qkv_a16, qk_b, qk_d, vt_b, vt_d, q, k, vt = _inproj(x2, lw, tables, bsz, seq)
    a_outs = _mixer_a((qkv_a, qkv_a4, qkv_a16), seq)
    ob = _mixer_b(qk_b, vt_b, lw["sink"], seq)
    oc = _mla_flash(q, k, vt).reshape(bsz * seq, GROUP_W)
    od = _mixer_d(qk_d, vt_d, lw["rpb"], seq)
    return _outproj_mlp(a_outs, ob, oc, od, x2, lw, gf, final, bsz, seq)


def _trunk(x, layer_ws, gf):
    bsz, seq, _ = x.shape
    tables = _rope_tables(seq)
    x2 = x.reshape(bsz * seq, D_MODEL)
    for i, lw in enumerate(layer_ws):
        x2 = _layer(x2, bsz, seq, lw, tables, gf, final=(i == len(layer_ws) - 1))
    return x2.reshape(bsz, seq, D_MODEL)


def kernel(x_prompt, x_sample, norm_attn, w_in, mla_q_norm, w_q_up, mla_kv_norm, w_kv_up, sink_logits, na_rpb, group_norm, w_out, norm_mlp, w_mlp_up, w_mlp_down, norm_final):
    params = (norm_attn, w_in, mla_q_norm, w_q_up, mla_kv_norm, w_kv_up, sink_logits, na_rpb,
              group_norm, w_out, norm_mlp, w_mlp_up, w_mlp_down)
    depth = norm_attn.shape[0]
    layer_ws = [_layer_weights(tuple(p[i] for p in params)) for i in range(depth)]
    gf = norm_final.reshape(1, -1)
    return (_trunk(x_prompt, layer_ws, gf), _trunk(x_sample, layer_ws, gf))
```

```python
import functools

import numpy as np
import jax
import jax.numpy as jnp
from jax import lax
from jax.experimental import pallas as pl
from jax.experimental.pallas import tpu as pltpu

F32 = jnp.float32
BF16 = jnp.bfloat16

D_MODEL = 1024
HEAD_DIM = 64
N_GROUPS = 4
GROUP_W = 256
A_CONFIGS = ((128, 1), (512, 4), (2048, 16))
B_HALF_WINDOW = 128
C_Q_RANK = 256
C_KV_RANK = 128
C_NOPE = 64
C_ROPE = 32
C_HEADS = 4
ROPE_THETA = 10000.0
GRID_W = 64
NA_KH = 8
NA_KW = 16
D_FF = 4096
EPS = 1e-5
NEG_INF = -1e30

W_A = 768
W_B = 384
W_D = 512
W_VT = 384
W_C = 640
MLA_HEAD_PAD = 128
MLA_V_ROWS = 80
LOG2E = 1.4426950408889634
LANES = 128

VMEM_LIMIT = 56 * 1024 * 1024


def _cparams(n_axes):
    return pltpu.CompilerParams(dimension_semantics=("arbitrary",) * n_axes,
                                vmem_limit_bytes=VMEM_LIMIT)


def _rms(x, g):
    return x * lax.rsqrt(jnp.mean(x * x, axis=-1, keepdims=True) + EPS) * g


def _mla_prep(c, ct, st, ctt, stt, qn, kvn, wqt_ref, wk_ref, wvt_ref, q_ref, k_ref, vt_ref):
    hp = MLA_HEAD_PAD
    cqn = _rms(c[:, :C_Q_RANK], qn).astype(BF16)
    qq = lax.dot_general(wqt_ref[...], cqn, (((1,), (1,)), ((), ())),
                         preferred_element_type=F32)
    scale = (C_NOPE + C_ROPE) ** -0.5 * LOG2E
    q_ref[0] = jnp.concatenate(
        [(qq[h * hp:(h + 1) * hp] * ctt + qq[(C_HEADS + h) * hp:(C_HEADS + h + 1) * hp] * stt) * scale
         for h in range(C_HEADS)], axis=0).astype(q_ref.dtype)
    ckvn = _rms(c[:, C_Q_RANK:C_Q_RANK + C_KV_RANK], kvn).astype(BF16)
    kk = jnp.dot(ckvn, wk_ref[...], preferred_element_type=F32)
    o0 = C_Q_RANK + C_KV_RANK
    kpe = c[:, o0:o0 + hp] * ct + c[:, o0 + hp:o0 + 2 * hp] * st
    k_ref[0] = jnp.concatenate([kk[:, h * hp:(h + 1) * hp] + kpe for h in range(C_HEADS)],
                               axis=1).astype(k_ref.dtype)
    vt = lax.dot_general(wvt_ref[...], ckvn, (((1,), (1,)), ((), ())), preferred_element_type=F32)
    ones = jnp.ones((MLA_V_ROWS - HEAD_DIM, vt.shape[1]), F32)
    vt_ref[0] = jnp.concatenate(
        [blk for h in range(C_HEADS) for blk in (vt[h * HEAD_DIM:(h + 1) * HEAD_DIM], ones)],
        axis=0).astype(vt_ref.dtype)


def _inproj_kernel(x_ref, g_ref, w_ref, wvt_ref, ct_ref, st_ref, ctt_ref, stt_ref, qn_ref, kvn_ref, wq_ref, wk_ref,
                   wcvt_ref,
                   oa_ref, oa4_ref, oa16_ref, ob_ref, od_ref, vtb_ref, vtd_ref, cq_ref, ck_ref, cvt_ref, a_sc):
    h = _rms(x_ref[...], g_ref[...]).astype(BF16)
    tm = h.shape[0]
    qkv_a = jnp.dot(h, w_ref[:, :W_A], preferred_element_type=F32)
    oa_ref[...] = qkv_a.astype(oa_ref.dtype)
    for c in range(W_A // LANES):
        a_sc[c] = qkv_a[:, c * LANES:(c + 1) * LANES]
    for dil, o_ref in ((4, oa4_ref), (16, oa16_ref)):
        for r in range(dil):
            o_ref[0, r] = jnp.concatenate(
                [a_sc[c, pl.ds(r, tm // dil, stride=dil), :] for c in range(W_A // LANES)],
                axis=1).astype(o_ref.dtype)
    off = W_A
    for o_ref in (ob_ref, od_ref):
        n = o_ref.shape[-1]
        o_ref[...] = jnp.dot(h, w_ref[:, off:off + n], preferred_element_type=F32).astype(o_ref.dtype)
        off += n
    c = jnp.dot(h, w_ref[:, off:off + W_C], preferred_element_type=F32)
    _mla_prep(c, ct_ref[...], st_ref[...], ctt_ref[...], stt_ref[...], qn_ref[...], kvn_ref[...],
              wq_ref, wk_ref, wcvt_ref, cq_ref, ck_ref, cvt_ref)
    vt = lax.dot_general(wvt_ref[...], h, (((1,), (1,)), ((), ())), preferred_element_type=F32)
    nb = vtb_ref.shape[0]
    vtb_ref[...] = vt[:nb].astype(vtb_ref.dtype)
    vtd_ref[...] = vt[nb:].astype(vtd_ref.dtype)


def _inproj(x2, lw, tables, bsz, seq, tm=512):
    t = x2.shape[0]
    nt = seq // tm
    widths = (W_B, W_D)
    vrows = (W_VT - GROUP_W, GROUP_W)
    hw = C_HEADS * MLA_HEAD_PAD
    cvrows = C_HEADS * MLA_V_ROWS
    tok = lambda n: pl.BlockSpec((tm, n), lambda b, i: (b * nt + i, 0))
    plane = lambda dil: pl.BlockSpec((1, dil, tm // dil, W_A), lambda b, i: (b, 0, i, 0))
    const = lambda shape: pl.BlockSpec(shape, lambda b, i: (0,) * len(shape))
    pos = pl.BlockSpec((tm, MLA_HEAD_PAD), lambda b, i: (i, 0))
    pos_t = pl.BlockSpec((MLA_HEAD_PAD, tm), lambda b, i: (0, i))
    return pl.pallas_call(
        _inproj_kernel,
        grid=(bsz, nt),
        in_specs=[tok(D_MODEL), const((1, D_MODEL)), const((D_MODEL, W_A + sum(widths) + W_C)),
                  const((W_VT, D_MODEL)), pos, pos, pos_t, pos_t, const((1, C_Q_RANK)), const((1, C_KV_RANK)),
                  const((2 * hw, C_Q_RANK)), const((C_KV_RANK, hw)), const((GROUP_W, C_KV_RANK))],
        out_specs=([tok(W_A), plane(4), plane(16)] + [tok(n) for n in widths]
                   + [pl.BlockSpec((r, tm), lambda b, i: (0, b * nt + i)) for r in vrows]
                   + [pl.BlockSpec((1, hw, tm), lambda b, i: (b, 0, i)),
                      pl.BlockSpec((1, tm, hw), lambda b, i: (b, i, 0)),
                      pl.BlockSpec((1, cvrows, tm), lambda b, i: (b, 0, i))]),
        out_shape=([jax.ShapeDtypeStruct((t, W_A), BF16)]
                   + [jax.ShapeDtypeStruct((bsz, dil, seq // dil, W_A), BF16) for dil in (4, 16)]
                   + [jax.ShapeDtypeStruct((t, n), BF16) for n in widths]
                   + [jax.ShapeDtypeStruct((r, t), BF16) for r in vrows]
                   + [jax.ShapeDtypeStruct((bsz, hw, seq), BF16), jax.ShapeDtypeStruct((bsz, seq, hw), BF16),
                      jax.ShapeDtypeStruct((bsz, cvrows, seq), BF16)]),
        scratch_shapes=[pltpu.VMEM((W_A // LANES, tm, LANES), F32)],
        compiler_params=_cparams(2),
        name="inproj",
    )(x2, lw["norm_attn"], lw["w_in"], lw["wvt_bd"], *tables, lw["qn"], lw["kvn"],
      lw["wq"], lw["wk"], lw["wvt"])


def _win_attn_t_kernel(*refs, heads, ck, tq, hb, n_tiles, has_sink, has_lse, v_rows):
    q_ref, kp_ref, km_ref, kn_ref, vp_ref, vm_ref, vn_ref, bias_ref = refs[:8]
    sink_ref = refs[8] if has_sink else None
    o_ref = refs[-2] if has_lse else refs[-1]
    lse_ref = refs[-1] if has_lse else None
    ts = q_ref.shape[0]
    tkw = tq + 2 * hb
    nh = len(heads)
    kw = jnp.concatenate([kp_ref[...], km_ref[...], kn_ref[...]], axis=0)
    if v_rows:
        vw = jnp.concatenate([vp_ref[...], vm_ref[...], vn_ref[...]], axis=0).astype(F32).T.astype(BF16)
    else:
        vw = jnp.concatenate([vp_ref[...], vm_ref[...], vn_ref[...]], axis=1)
    row = lax.broadcasted_iota(jnp.int32, (ck, tq), 0)
    masks = [(row >= lo) & (row < lo + HEAD_DIM) for (_, lo, _, _) in heads]
    qt_all = q_ref[...].astype(F32).T.astype(BF16)
    ones = jnp.ones((MLA_V_ROWS - HEAD_DIM, tkw), vw.dtype)
    step = pl.program_id(0)
    for j in range(ts // tq):
        tile = (step * (ts // tq) + j) % n_tiles
        kind = jnp.where(tile == 0, 0, jnp.where(tile == n_tiles - 1, 2, 1))
        qt = qt_all[:, j * tq:(j + 1) * tq]
        q_stack = jnp.concatenate(
            [jnp.where(mk, qt[g * ck:(g + 1) * ck, :], jnp.zeros((), qt.dtype))
             for (g, _, _, _), mk in zip(heads, masks)], axis=1)
        st = jnp.dot(kw[j * tq:j * tq + tkw], q_stack, preferred_element_type=F32) + bias_ref[kind]
        m = jnp.max(st, axis=0, keepdims=True)
        if has_sink:
            sink = jnp.concatenate([jnp.full((1, tq), sink_ref[i] * LOG2E, F32) for i in range(nh)], axis=1)
            m = jnp.maximum(m, sink)
            sink_p = jnp.exp2(sink - m)
        pt = jnp.exp2(st - m).astype(BF16)
        outs = [None] * nh
        lses = [None] * nh
        for i, (_, _, vrow, out_pos) in enumerate(heads):
            lhs = jnp.concatenate([vw[vrow:vrow + HEAD_DIM, j * tq:j * tq + tkw], ones], axis=0)
            r = jnp.dot(lhs, pt[:, i * tq:(i + 1) * tq], preferred_element_type=F32)
            den = r[HEAD_DIM:HEAD_DIM + 1]
            if has_sink:
                den = den + sink_p[:, i * tq:(i + 1) * tq]
            outs[out_pos] = r[:HEAD_DIM] / den
            if has_lse:
                lse = (m[:, i * tq:(i + 1) * tq] + jnp.log2(den)) * (1.0 / LOG2E)
                lses[out_pos] = jnp.broadcast_to(lse, (HEAD_DIM, tq))
        o_ref[j * tq:(j + 1) * tq, :] = jnp.concatenate(outs, axis=0).T.astype(o_ref.dtype)
        if has_lse:
            lse_ref[j * tq:(j + 1) * tq, :] = jnp.concatenate(lses, axis=0).T


def _win_attn_t(qk, vt, bias, sink, *, seq, ts, tq, hb, cq, ck, qcol, kcol, vcol, heads, has_lse, name):
    rows = qk.shape[0]
    ts = min(ts, rows)
    tq = min(tq, seq)
    per = ts // hb
    nhb = rows // hb
    nh = len(heads)
    tkw = tq + 2 * hb
    v_rows = vt is None

    def prev_idx(i):
        return jnp.maximum(i * per - 1, 0)

    def next_idx(i):
        return jnp.minimum((i + 1) * per, nhb - 1)

    in_specs = [
        pl.BlockSpec((ts, cq), lambda i: (i, qcol)),
        pl.BlockSpec((hb, ck), lambda i: (prev_idx(i), kcol)),
        pl.BlockSpec((ts, ck), lambda i: (i, kcol)),
        pl.BlockSpec((hb, ck), lambda i: (next_idx(i), kcol)),
    ]
    if v_rows:
        in_specs += [
            pl.BlockSpec((hb, GROUP_W), lambda i: (prev_idx(i), vcol)),
            pl.BlockSpec((ts, GROUP_W), lambda i: (i, vcol)),
            pl.BlockSpec((hb, GROUP_W), lambda i: (next_idx(i), vcol)),
        ]
        args = [qk] * 7
    else:
        cv = vt.shape[0]
        in_specs += [
            pl.BlockSpec((cv, hb), lambda i: (0, prev_idx(i))),
            pl.BlockSpec((cv, ts), lambda i: (0, i)),
            pl.BlockSpec((cv, hb), lambda i: (0, next_idx(i))),
        ]
        args = [qk] * 4 + [vt] * 3
    in_specs.append(pl.BlockSpec((3, tkw, nh * tq), lambda i: (0, 0, 0)))
    args.append(bias)
    if sink is not None:
        in_specs.append(pl.BlockSpec(memory_space=pltpu.SMEM))
        args.append(sink)
    out_specs = [pl.BlockSpec((ts, GROUP_W), lambda i: (i, 0))]
    out_shape = [jax.ShapeDtypeStruct((rows, GROUP_W), BF16)]
    if has_lse:
        out_specs.append(pl.BlockSpec((ts, GROUP_W), lambda i: (i, 0)))
        out_shape.append(jax.ShapeDtypeStruct((rows, GROUP_W), F32))
    return pl.pallas_call(
        functools.partial(_win_attn_t_kernel, heads=heads, ck=ck, tq=tq, hb=hb, n_tiles=seq // tq,
                          has_sink=sink is not None, has_lse=has_lse, v_rows=v_rows),
        grid=(rows // ts,),
        in_specs=in_specs,
        out_specs=out_specs,
        out_shape=out_shape,
        compiler_params=_cparams(1),
        name=name,
    )(*args)


def _transpose_bias(bias):
    return jnp.swapaxes(bias, 1, 2) * LOG2E


def _tile_indices(nt):
    return (0, min(1, nt - 1), nt - 1)


def _banded_bias(seq, tq, hb, hw, slopes, dist_scale):
    nt = seq // tq
    tabs = []
    for ti in _tile_indices(nt):
        qpos = ti * tq + np.arange(tq)[:, None]
        kpos = ti * tq - hb + np.arange(tq + 2 * hb)[None, :]
        dist = np.abs(kpos - qpos)
        valid = (dist <= hw) & (kpos >= 0) & (kpos < seq)
        pen = -np.asarray(slopes, np.float32)[:, None, None] * (dist.astype(np.float32) * np.float32(dist_scale))[None]
        tabs.append(np.where(valid[None], pen, np.float32(NEG_INF)).reshape(-1, tq + 2 * hb))
    return jnp.asarray(np.stack(tabs).astype(np.float32))


def _na_bias(rpb, seq, rows_per_tile):
    rows = seq // GRID_W
    kh = min(NA_KH, rows)
    nt = rows // rows_per_tile
    tq = rows_per_tile * GRID_W
    cq = np.arange(GRID_W)[:, None]
    ck = np.arange(GRID_W)[None, :]
    col_start = np.clip(cq - NA_KW // 2, 0, GRID_W - NA_KW)
    col_ok = (ck >= col_start) & (ck < col_start + NA_KW)
    dc = np.clip(ck - cq, -(NA_KW - 1), NA_KW - 1) + (NA_KW - 1)
    col_hot = jnp.asarray(np.eye(2 * NA_KW - 1, dtype=np.float32)[dc])
    tabs = []
    for ti in _tile_indices(nt):
        rq = (ti * rows_per_tile + np.arange(rows_per_tile))[:, None]
        rk = ((ti - 1) * rows_per_tile + np.arange(3 * rows_per_tile))[None, :]
        row_start = np.clip(rq - kh // 2, 0, rows - kh)
        row_ok = (rk >= row_start) & (rk < row_start + kh)
        dr = np.clip(rk - rq + (NA_KH - 1), 0, 2 * NA_KH - 2)
        row_hot = jnp.asarray(np.eye(2 * NA_KH - 1, dtype=np.float32)[dr])
        vals = jnp.einsum("abr,hrc,xyc->haxby", row_hot, rpb.astype(F32), col_hot,
                          precision=lax.Precision.HIGHEST)
        valid = row_ok[:, None, :, None] & col_ok[None, :, None, :]
        tabs.append(jnp.where(valid[None], vals, NEG_INF).reshape(-1, 3 * tq))
    return jnp.stack(tabs)


def _alibi_slopes():
    s = np.exp2(-8.0 * np.arange(1, 9, dtype=np.float32) / 8.0)
    return s[1::2], s[0::2]


GQA_HEAD_ORDER = (0, 2, 1, 3)
HEADS_T_FULL = tuple((0, 64 * h, 64 * h, h) for h in range(4))
HEADS_T_GQA = ((0, 0, 0, 0), (0, 64, 64, 2), (1, 0, 0, 1), (1, 64, 64, 3))

A_HALO = 64


def _mixer_a(qkv_planes, seq):
    slopes_a, _ = _alibi_slopes()
    outs = []
    for (window, dil), qkv in zip(A_CONFIGS, qkv_planes):
        length = seq // dil
        hw = window // (2 * dil)
        tq = min(128, length)
        bias = _banded_bias(length, tq, A_HALO, hw, slopes_a, float(dil))
        outs.append(_win_attn_t(
            qkv.reshape(-1, W_A), None, _transpose_bias(bias), None, seq=length, ts=4096, tq=tq, hb=A_HALO,
            cq=GROUP_W, ck=GROUP_W, qcol=0, kcol=1, vcol=2, heads=HEADS_T_FULL, has_lse=True,
            name=f"mix_a_d{dil}"))
    return outs


def _mixer_b(qk_b, vt_b, sink, seq):
    _, slopes_b = _alibi_slopes()
    tq = 256
    bias = _banded_bias(seq, tq, B_HALF_WINDOW, B_HALF_WINDOW, slopes_b[list(GQA_HEAD_ORDER)], 1.0)
    (o,) = _win_attn_t(
        qk_b, vt_b, _transpose_bias(bias), sink, seq=seq, ts=4096, tq=tq, hb=B_HALF_WINDOW,
        cq=GROUP_W, ck=128, qcol=0, kcol=2, vcol=None, heads=HEADS_T_GQA, has_lse=False, name="mix_b")
    return o


def _mixer_d(qk_d, vt_d, rpb, seq):
    rows_per_tile = 4
    tq = rows_per_tile * GRID_W
    bias = _na_bias(rpb, seq, rows_per_tile)
    (o,) = _win_attn_t(
        qk_d, vt_d, _transpose_bias(bias), None, seq=seq, ts=4096, tq=tq, hb=tq,
        cq=GROUP_W, ck=GROUP_W, qcol=0, kcol=1, vcol=None, heads=HEADS_T_FULL, has_lse=False, name="mix_d")
    return o


def _mla_flash_kernel(q_ref, k_ref, vt_ref, o_ref, m_sc, acc_sc, s_sc, *, kc, depth):
    hp, hv = MLA_HEAD_PAD, MLA_V_ROWS
    kv = pl.program_id(2)
    tk = k_ref.shape[1]

    @pl.when(kv == 0)
    def _():
        m_sc[...] = jnp.full(m_sc.shape, NEG_INF, F32)
        acc_sc[...] = jnp.zeros(acc_sc.shape, F32)

    units = [(c, h) for c in range(tk // kc) for h in range(C_HEADS)]

    def scores(u):
        c, h = units[u]
        s_sc[u % depth] = jnp.dot(k_ref[0, c * kc:(c + 1) * kc, h * hp:(h + 1) * hp],
                                  q_ref[0, h * hp:(h + 1) * hp, :], preferred_element_type=F32)

    for u in range(min(depth, len(units))):
        scores(u)
    for u, (c, h) in enumerate(units):
        st = s_sc[u % depth]
        m_old = m_sc[h:h + 1, :]
        m_new = jnp.maximum(m_old, jnp.max(st, axis=0, keepdims=True))
        alpha = jnp.exp2(m_old - m_new)
        pt = jnp.exp2(st - m_new).astype(BF16)
        m_sc[h:h + 1, :] = m_new
        if u + depth < len(units):
            scores(u + depth)
        rows = slice(h * hv, (h + 1) * hv)
        acc_sc[rows, :] = alpha * acc_sc[rows, :] + jnp.dot(
            vt_ref[0, rows, c * kc:(c + 1) * kc], pt, preferred_element_type=F32)

    @pl.when(kv == pl.num_programs(2) - 1)
    def _():
        ot = jnp.concatenate(
            [acc_sc[h * hv:h * hv + HEAD_DIM, :] / acc_sc[h * hv + HEAD_DIM:h * hv + HEAD_DIM + 1, :]
             for h in range(C_HEADS)], axis=0)
        o_ref[0] = ot.T.astype(o_ref.dtype)


def _mla_flash(q, k, vt, tq=512, tk=4096, kc=256, depth=8):
    b, hw, s = q.shape
    tq = min(tq, s)
    tk = min(tk, s)
    vrows = C_HEADS * MLA_V_ROWS
    return pl.pallas_call(
        functools.partial(_mla_flash_kernel, kc=kc, depth=depth),
        grid=(b, s // tq, s // tk),
        in_specs=[pl.BlockSpec((1, hw, tq), lambda n, i, j: (n, 0, i)),
                  pl.BlockSpec((1, tk, hw), lambda n, i, j: (n, j, 0)),
                  pl.BlockSpec((1, vrows, tk), lambda n, i, j: (n, 0, j))],
        out_specs=pl.BlockSpec((1, tq, GROUP_W), lambda n, i, j: (n, i, 0)),
        out_shape=jax.ShapeDtypeStruct((b, s, GROUP_W), BF16),
        scratch_shapes=[pltpu.VMEM((8, tq), F32), pltpu.VMEM((vrows, tq), F32),
                        pltpu.VMEM((depth, kc, tq), F32)],
        compiler_params=_cparams(3),
        name="mla_flash",
    )(q, k, vt)


def _rope_tables(seq):
    inv_freq = 1.0 / (ROPE_THETA ** (jnp.arange(0, C_ROPE, 2, dtype=F32) / C_ROPE))
    ang = jnp.arange(seq, dtype=F32)[:, None] * inv_freq[None, :]
    cos, sin = jnp.cos(ang), jnp.sin(ang)
    pad = MLA_HEAD_PAD - C_NOPE - C_ROPE
    ct = jnp.concatenate([jnp.ones((seq, C_NOPE), F32), cos, cos, jnp.zeros((seq, pad), F32)], axis=1)
    st = jnp.concatenate([jnp.zeros((seq, C_NOPE), F32), -sin, sin, jnp.zeros((seq, pad), F32)], axis=1)
    return ct, st, ct.T, st.T


def _outproj_kernel(a1_ref, a2_ref, a3_ref, l1_ref, l2_ref, l3_ref, ob_ref, oc_ref, od_ref,
                    x_ref, g_ref, w_ref, gm_ref, wu_ref, wd_ref, gf_ref, o_ref, *scratch, final, ff_chunk):
    tm = x_ref.shape[0]
    ncol = GROUP_W // LANES

    def natural(ref, dil, sc):
        for r in range(dil):
            for c in range(ncol):
                sc[c, pl.ds(r, tm // dil, stride=dil), :] = ref[0, r, :, c * LANES:(c + 1) * LANES].astype(F32)
        return jnp.concatenate([sc[c] for c in range(ncol)], axis=1)

    a_s = [a1_ref[...].astype(F32), natural(a2_ref, 4, scratch[0]), natural(a3_ref, 16, scratch[1])]
    ls = [l1_ref[...], natural(l2_ref, 4, scratch[2]), natural(l3_ref, 16, scratch[3])]
    m = jnp.maximum(jnp.maximum(ls[0], ls[1]), ls[2])
    es = [jnp.exp(l - m) for l in ls]
    den = es[0] + es[1] + es[2]
    oa = sum((e / den) * a for e, a in zip(es, a_s))
    groups = [oa, ob_ref[...].astype(F32), oc_ref[...].astype(F32), od_ref[...].astype(F32)]
    y = jnp.concatenate([_rms(o, g_ref[i:i + 1, :]).astype(BF16) for i, o in enumerate(groups)], axis=1)
    x = x_ref[...] + jnp.dot(y, w_ref[...], preferred_element_type=F32)
    h = _rms(x, gm_ref[...]).astype(BF16)
    acc = x
    for c in range(D_FF // ff_chunk):
        cols = slice(c * ff_chunk, (c + 1) * ff_chunk)
        u = jnp.dot(h, wu_ref[:, cols], preferred_element_type=F32)
        u = jnp.square(jnp.maximum(u, 0.0)).astype(BF16)
        acc = acc + jnp.dot(u, wd_ref[cols, :], preferred_element_type=F32)
    if final:
        acc = _rms(acc, gf_ref[...])
    o_ref[...] = acc


def _outproj_mlp(a_outs, ob, oc, od, x2, lw, gf, final, bsz, seq, tm=512, ff_chunk=1024):
    t = x2.shape[0]
    nt = seq // tm
    tok = lambda n: pl.BlockSpec((tm, n), lambda b, i: (b * nt + i, 0))
    plane = lambda dil: pl.BlockSpec((1, dil, tm // dil, GROUP_W), lambda b, i: (b, 0, i, 0))
    dils = [dil for _, dil in A_CONFIGS]
    a_specs = [tok(GROUP_W) if dil == 1 else plane(dil) for dil in dils]
    planes = lambda arr, dil: arr if dil == 1 else arr.reshape(bsz, dil, seq // dil, GROUP_W)
    args = ([planes(o, dil) for (o, _), dil in zip(a_outs, dils)] + [planes(l, dil) for (_, l), dil in zip(a_outs, dils)]
            + [ob, oc, od, x2, lw["gn"], lw["w_out"], lw["norm_mlp"], lw["w_up"], lw["w_down"], gf])
    const = lambda shape: pl.BlockSpec(shape, lambda b, i: (0,) * len(shape))
    return pl.pallas_call(
        functools.partial(_outproj_kernel, final=final, ff_chunk=ff_chunk),
        grid=(bsz, nt),
        in_specs=a_specs + a_specs + [tok(GROUP_W)] * 3 + [
            tok(D_MODEL), const((N_GROUPS, GROUP_W)), const((D_MODEL, D_MODEL)),
            const((1, D_MODEL)), const((D_MODEL, D_FF)), const((D_FF, D_MODEL)), const((1, D_MODEL))],
        out_specs=tok(D_MODEL),
        out_shape=jax.ShapeDtypeStruct((t, D_MODEL), F32),
        scratch_shapes=[pltpu.VMEM((GROUP_W // LANES, tm, LANES), F32) for _ in range(4)],
        compiler_params=_cparams(2),
        name="outproj_mlp_final" if final else "outproj_mlp",
    )(*args)


def _gqa_perm():
    return np.concatenate([np.arange(h * HEAD_DIM, (h + 1) * HEAD_DIM) for h in GQA_HEAD_ORDER])


def _layer_weights(lp):
    (norm_attn, w_in, mla_q_norm, w_q_up, mla_kv_norm, w_kv_up, sink_logits, na_rpb,
     group_norm, w_out, norm_mlp, w_mlp_up, w_mlp_down) = lp
    perm = _gqa_perm()
    half = C_ROPE // 2
    z = lambda n: jnp.zeros((D_MODEL, n), w_in.dtype)
    kr = w_in[:, 1664:1696]
    kr_sw = jnp.concatenate([kr[:, half:], kr[:, :half]], axis=1)
    pad = MLA_HEAD_PAD - C_NOPE - C_ROPE
    qscale = HEAD_DIM ** -0.5 * LOG2E
    w_in2 = jnp.concatenate([
        w_in[:, 0:256] * qscale, w_in[:, 256:768],
        w_in[:, 768:1024][:, perm] * qscale, w_in[:, 1024:1152],
        w_in[:, 1696:1952] * qscale, w_in[:, 1952:2208],
        w_in[:, 1280:1664],
        z(C_NOPE), kr, z(pad), z(C_NOPE), kr_sw, z(pad),
    ], axis=1).astype(BF16)

    wq = w_q_up.reshape(C_Q_RANK, C_HEADS, C_NOPE + C_ROPE)
    zq = lambda n: jnp.zeros((C_Q_RANK, C_HEADS, n), w_q_up.dtype)
    rope = wq[:, :, C_NOPE:]
    rope_sw = jnp.concatenate([rope[:, :, half:], rope[:, :, :half]], axis=2)
    wq_main = jnp.concatenate([wq, zq(pad)], axis=2).reshape(C_Q_RANK, -1)
    wq_swap = jnp.concatenate([zq(C_NOPE), rope_sw, zq(pad)], axis=2).reshape(C_Q_RANK, -1)
    wq2 = jnp.concatenate([wq_main, wq_swap], axis=1).T.astype(BF16)

    wkv = w_kv_up.reshape(C_KV_RANK, C_HEADS, C_NOPE + HEAD_DIM)
    wk2 = jnp.concatenate([wkv[:, :, :C_NOPE], jnp.zeros((C_KV_RANK, C_HEADS, MLA_HEAD_PAD - C_NOPE), w_kv_up.dtype)],
                          axis=2).reshape(C_KV_RANK, -1).astype(BF16)
    wvt = wkv[:, :, C_NOPE:].reshape(C_KV_RANK, -1).T.astype(BF16)

    wvt_bd = jnp.concatenate([w_in[:, 1152:1280], w_in[:, 2208:2464]], axis=1).T.astype(BF16)
    return dict(
        norm_attn=norm_attn.reshape(1, -1), w_in=w_in2, wvt_bd=wvt_bd,
        qn=mla_q_norm.reshape(1, -1), kvn=mla_kv_norm.reshape(1, -1), wq=wq2, wk=wk2, wvt=wvt,
        sink=sink_logits[jnp.asarray(GQA_HEAD_ORDER)], rpb=na_rpb,
        gn=group_norm, w_out=w_out.astype(BF16), norm_mlp=norm_mlp.reshape(1, -1),
        w_up=w_mlp_up.astype(BF16), w_down=w_mlp_down.astype(BF16))


def _layer(x2, bsz, seq, lw, tables, gf, final):
    qkv_a, qkv_a4, qkv_a16, qk_b, qk_d, vt_b, vt_d, q, k, vt = _inproj(x2, lw, tables, bsz, seq)
    a_outs = _mixer_a((qkv_a, qkv_a4, qkv_a16), seq)
    ob = _mixer_b(qk_b, vt_b, lw["sink"], seq)
    oc = _mla_flash(q, k, vt).reshape(bsz * seq, GROUP_W)
    od = _mixer_d(qk_d, vt_d, lw["rpb"], seq)
    return _outproj_mlp(a_outs, ob, oc, od, x2, lw, gf, final, bsz, seq)


def _trunk(x, layer_ws, gf):
    bsz, seq, _ = x.shape
    tables = _rope_tables(seq)
    x2 = x.reshape(bsz * seq, D_MODEL)
    for i, lw in enumerate(layer_ws):
        x2 = _layer(x2, bsz, seq, lw, tables, gf, final=(i == len(layer_ws) - 1))
    return x2.reshape(bsz, seq, D_MODEL)


def kernel(x_prompt, x_sample, norm_attn, w_in, mla_q_norm, w_q_up, mla_kv_norm, w_kv_up, sink_logits, na_rpb, group_norm, w_out, norm_mlp, w_mlp_up, w_mlp_down, norm_final):
    params = (norm_attn, w_in, mla_q_norm, w_q_up, mla_kv_norm, w_kv_up, sink_logits, na_rpb,
              group_norm, w_out, norm_mlp, w_mlp_up, w_mlp_down)
    depth = norm_attn.shape[0]
    layer_ws = [_layer_weights(tuple(p[i] for p in params)) for i in range(depth)]
    gf = norm_final.reshape(1, -1)
    return (_trunk(x_prompt, layer_ws, gf), _trunk(x_sample, layer_ws, gf))
```

```python
import functools

import numpy as np
import jax
import jax.numpy as jnp
from jax import lax
from jax.experimental import pallas as pl
from jax.experimental.pallas import tpu as pltpu

F32 = jnp.float32
BF16 = jnp.bfloat16

D_MODEL = 1024
HEAD_DIM = 64
N_GROUPS = 4
GROUP_W = 256
A_CONFIGS = ((128, 1), (512, 4), (2048, 16))
B_HALF_WINDOW = 128
C_Q_RANK = 256
C_KV_RANK = 128
C_NOPE = 64
C_ROPE = 32
C_HEADS = 4
ROPE_THETA = 10000.0
GRID_W = 64
NA_KH = 8
NA_KW = 16
D_FF = 4096
EPS = 1e-5
NEG_INF = -1e30

W_A = 768
W_B = 384
W_D = 512
W_VT = 384
W_C = 640
MLA_HEAD_PAD = 128
MLA_V_ROWS = 80
LOG2E = 1.4426950408889634
LANES = 128

VMEM_LIMIT = 56 * 1024 * 1024


def _cparams(n_axes):
    return pltpu.CompilerParams(dimension_semantics=("arbitrary",) * n_axes,
                                vmem_limit_bytes=VMEM_LIMIT)


def _rms(x, g):
    return x * lax.rsqrt(jnp.mean(x * x, axis=-1, keepdims=True) + EPS) * g


def _mla_prep(c, ct, st, ctt, stt, qn, kvn, wqt_ref, wk_ref, wvt_ref, q_ref, k_ref, vt_ref):
    hp = MLA_HEAD_PAD
    cqn = _rms(c[:, :C_Q_RANK], qn).astype(BF16)
    qq = jnp.dot(wqt_ref[...], _rms(c[:, :C_Q_RANK], qn).T.astype(BF16),
                 preferred_element_type=F32)
    scale = (C_NOPE + C_ROPE) ** -0.5 * LOG2E
    q_ref[0] = jnp.concatenate(
        [(qq[h * hp:(h + 1) * hp] * ctt + qq[(C_HEADS + h) * hp:(C_HEADS + h + 1) * hp] * stt) * scale
         for h in range(C_HEADS)], axis=0).astype(q_ref.dtype)
    ckvn = _rms(c[:, C_Q_RANK:C_Q_RANK + C_KV_RANK], kvn).astype(BF16)
    kk = jnp.dot(ckvn, wk_ref[...], preferred_element_type=F32)
    o0 = C_Q_RANK + C_KV_RANK
    kpe = c[:, o0:o0 + hp] * ct + c[:, o0 + hp:o0 + 2 * hp] * st
    k_ref[0] = jnp.concatenate([kk[:, h * hp:(h + 1) * hp] + kpe for h in range(C_HEADS)],
                               axis=1).astype(k_ref.dtype)
    vt = jnp.dot(wvt_ref[...], _rms(c[:, C_Q_RANK:C_Q_RANK + C_KV_RANK], kvn).T.astype(BF16),
                 preferred_element_type=F32)
    ones = jnp.ones((MLA_V_ROWS - HEAD_DIM, vt.shape[1]), F32)
    vt_ref[0] = jnp.concatenate(
        [blk for h in range(C_HEADS) for blk in (vt[h * HEAD_DIM:(h + 1) * HEAD_DIM], ones)],
        axis=0).astype(vt_ref.dtype)


def _inproj_kernel(x_ref, g_ref, w_ref, wvt_ref, ct_ref, st_ref, ctt_ref, stt_ref, qn_ref, kvn_ref, wq_ref, wk_ref,
                   wcvt_ref,
                   oa_ref, oa4_ref, oa16_ref, ob_ref, od_ref, vtb_ref, vtd_ref, cq_ref, ck_ref, cvt_ref, a_sc):
    hf = _rms(x_ref[...], g_ref[...])
    h = hf.astype(BF16)
    tm = h.shape[0]
    qkv_a = jnp.dot(h, w_ref[:, :W_A], preferred_element_type=F32)
    oa_ref[...] = qkv_a.astype(oa_ref.dtype)
    for c in range(W_A // LANES):
        a_sc[c] = qkv_a[:, c * LANES:(c + 1) * LANES]
    for dil, o_ref in ((4, oa4_ref), (16, oa16_ref)):
        for r in range(dil):
            o_ref[0, r] = jnp.concatenate(
                [a_sc[c, pl.ds(r, tm // dil, stride=dil), :] for c in range(W_A // LANES)],
                axis=1).astype(o_ref.dtype)
    off = W_A
    for o_ref in (ob_ref, od_ref):
        n = o_ref.shape[-1]
        o_ref[...] = jnp.dot(h, w_ref[:, off:off + n], preferred_element_type=F32).astype(o_ref.dtype)
        off += n
    c = jnp.dot(h, w_ref[:, off:off + W_C], preferred_element_type=F32)
    _mla_prep(c, ct_ref[...], st_ref[...], ctt_ref[...], stt_ref[...], qn_ref[...], kvn_ref[...],
              wq_ref, wk_ref, wcvt_ref, cq_ref, ck_ref, cvt_ref)
    vt = jnp.dot(wvt_ref[...], hf.T.astype(BF16), preferred_element_type=F32)
    nb = vtb_ref.shape[0]
    vtb_ref[...] = vt[:nb].astype(vtb_ref.dtype)
    vtd_ref[...] = vt[nb:].astype(vtd_ref.dtype)


def _inproj(x2, lw, tables, bsz, seq, tm=512):
    t = x2.shape[0]
    nt = seq // tm
    widths = (W_B, W_D)
    vrows = (W_VT - GROUP_W, GROUP_W)
    hw = C_HEADS * MLA_HEAD_PAD
    cvrows = C_HEADS * MLA_V_ROWS
    tok = lambda n: pl.BlockSpec((tm, n), lambda b, i: (b * nt + i, 0))
    plane = lambda dil: pl.BlockSpec((1, dil, tm // dil, W_A), lambda b, i: (b, 0, i, 0))
    const = lambda shape: pl.BlockSpec(shape, lambda b, i: (0,) * len(shape))
    pos = pl.BlockSpec((tm, MLA_HEAD_PAD), lambda b, i: (i, 0))
    pos_t = pl.BlockSpec((MLA_HEAD_PAD, tm), lambda b, i: (0, i))
    return pl.pallas_call(
        _inproj_kernel,
        grid=(bsz, nt),
        in_specs=[tok(D_MODEL), const((1, D_MODEL)), const((D_MODEL, W_A + sum(widths) + W_C)),
                  const((W_VT, D_MODEL)), pos, pos, pos_t, pos_t, const((1, C_Q_RANK)), const((1, C_KV_RANK)),
                  const((2 * hw, C_Q_RANK)), const((C_KV_RANK, hw)), const((GROUP_W, C_KV_RANK))],
        out_specs=([tok(W_A), plane(4), plane(16)] + [tok(n) for n in widths]
                   + [pl.BlockSpec((r, tm), lambda b, i: (0, b * nt + i)) for r in vrows]
                   + [pl.BlockSpec((1, hw, tm), lambda b, i: (b, 0, i)),
                      pl.BlockSpec((1, tm, hw), lambda b, i: (b, i, 0)),
                      pl.BlockSpec((1, cvrows, tm), lambda b, i: (b, 0, i))]),
        out_shape=([jax.ShapeDtypeStruct((t, W_A), BF16)]
                   + [jax.ShapeDtypeStruct((bsz, dil, seq // dil, W_A), BF16) for dil in (4, 16)]
                   + [jax.ShapeDtypeStruct((t, n), BF16) for n in widths]
                   + [jax.ShapeDtypeStruct((r, t), BF16) for r in vrows]
                   + [jax.ShapeDtypeStruct((bsz, hw, seq), BF16), jax.ShapeDtypeStruct((bsz, seq, hw), BF16),
                      jax.ShapeDtypeStruct((bsz, cvrows, seq), BF16)]),
        scratch_shapes=[pltpu.VMEM((W_A // LANES, tm, LANES), F32)],
        compiler_params=_cparams(2),
        name="inproj",
    )(x2, lw["norm_attn"], lw["w_in"], lw["wvt_bd"], *tables, lw["qn"], lw["kvn"],
      lw["wq"], lw["wk"], lw["wvt"])


def _win_attn_t_kernel(*refs, heads, ck, tq, hb, n_tiles, has_sink, has_lse, v_rows):
    q_ref, kp_ref, km_ref, kn_ref, vp_ref, vm_ref, vn_ref, bias_ref = refs[:8]
    sink_ref = refs[8] if has_sink else None
    o_ref = refs[-2] if has_lse else refs[-1]
    lse_ref = refs[-1] if has_lse else None
    ts = q_ref.shape[0]
    tkw = tq + 2 * hb
    nh = len(heads)
    kw = jnp.concatenate([kp_ref[...], km_ref[...], kn_ref[...]], axis=0)
    if v_rows:
        vw = jnp.concatenate([vp_ref[...], vm_ref[...], vn_ref[...]], axis=0).astype(F32).T.astype(BF16)
    else:
        vw = jnp.concatenate([vp_ref[...], vm_ref[...], vn_ref[...]], axis=1)
    row = lax.broadcasted_iota(jnp.int32, (ck, tq), 0)
    masks = [(row >= lo) & (row < lo + HEAD_DIM) for (_, lo, _, _) in heads]
    qt_all = q_ref[...].astype(F32).T.astype(BF16)
    ones = jnp.ones((MLA_V_ROWS - HEAD_DIM, tkw), vw.dtype)
    step = pl.program_id(0)
    for j in range(ts // tq):
        tile = (step * (ts // tq) + j) % n_tiles
        kind = jnp.where(tile == 0, 0, jnp.where(tile == n_tiles - 1, 2, 1))
        qt = qt_all[:, j * tq:(j + 1) * tq]
        q_stack = jnp.concatenate(
            [jnp.where(mk, qt[g * ck:(g + 1) * ck, :], jnp.zeros((), qt.dtype))
             for (g, _, _, _), mk in zip(heads, masks)], axis=1)
        st = jnp.dot(kw[j * tq:j * tq + tkw], q_stack, preferred_element_type=F32) + bias_ref[kind]
        m = jnp.max(st, axis=0, keepdims=True)
        if has_sink:
            sink = jnp.concatenate([jnp.full((1, tq), sink_ref[i] * LOG2E, F32) for i in range(nh)], axis=1)
            m = jnp.maximum(m, sink)
            sink_p = jnp.exp2(sink - m)
        pt = jnp.exp2(st - m).astype(BF16)
        outs = [None] * nh
        lses = [None] * nh
        for i, (_, _, vrow, out_pos) in enumerate(heads):
            lhs = jnp.concatenate([vw[vrow:vrow + HEAD_DIM, j * tq:j * tq + tkw], ones], axis=0)
            r = jnp.dot(lhs, pt[:, i * tq:(i + 1) * tq], preferred_element_type=F32)
            den = r[HEAD_DIM:HEAD_DIM + 1]
            if has_sink:
                den = den + sink_p[:, i * tq:(i + 1) * tq]
            outs[out_pos] = r[:HEAD_DIM] / den
            if has_lse:
                lse = (m[:, i * tq:(i + 1) * tq] + jnp.log2(den)) * (1.0 / LOG2E)
                lses[out_pos] = jnp.broadcast_to(lse, (HEAD_DIM, tq))
        o_ref[j * tq:(j + 1) * tq, :] = jnp.concatenate(outs, axis=0).T.astype(o_ref.dtype)
        if has_lse:
            lse_ref[j * tq:(j + 1) * tq, :] = jnp.concatenate(lses, axis=0).T


def _win_attn_t(qk, vt, bias, sink, *, seq, ts, tq, hb, cq, ck, qcol, kcol, vcol, heads, has_lse, name):
    rows = qk.shape[0]
    ts = min(ts, rows)
    tq = min(tq, seq)
    per = ts // hb
    nhb = rows // hb
    nh = len(heads)
    tkw = tq + 2 * hb
    v_rows = vt is None

    def prev_idx(i):
        return jnp.maximum(i * per - 1, 0)

    def next_idx(i):
        return jnp.minimum((i + 1) * per, nhb - 1)

    in_specs = [
        pl.BlockSpec((ts, cq), lambda i: (i, qcol)),
        pl.BlockSpec((hb, ck), lambda i: (prev_idx(i), kcol)),
        pl.BlockSpec((ts, ck), lambda i: (i, kcol)),
        pl.BlockSpec((hb, ck), lambda i: (next_idx(i), kcol)),
    ]
    if v_rows:
        in_specs += [
            pl.BlockSpec((hb, GROUP_W), lambda i: (prev_idx(i), vcol)),
            pl.BlockSpec((ts, GROUP_W), lambda i: (i, vcol)),
            pl.BlockSpec((hb, GROUP_W), lambda i: (next_idx(i), vcol)),
        ]
        args = [qk] * 7
    else:
        cv = vt.shape[0]
        in_specs += [
            pl.BlockSpec((cv, hb), lambda i: (0, prev_idx(i))),
            pl.BlockSpec((cv, ts), lambda i: (0, i)),
            pl.BlockSpec((cv, hb), lambda i: (0, next_idx(i))),
        ]
        args = [qk] * 4 + [vt] * 3
    in_specs.append(pl.BlockSpec((3, tkw, nh * tq), lambda i: (0, 0, 0)))
    args.append(bias)
    if sink is not None:
        in_specs.append(pl.BlockSpec(memory_space=pltpu.SMEM))
        args.append(sink)
    out_specs = [pl.BlockSpec((ts, GROUP_W), lambda i: (i, 0))]
    out_shape = [jax.ShapeDtypeStruct((rows, GROUP_W), BF16)]
    if has_lse:
        out_specs.append(pl.BlockSpec((ts, GROUP_W), lambda i: (i, 0)))
        out_shape.append(jax.ShapeDtypeStruct((rows, GROUP_W), F32))
    return pl.pallas_call(
        functools.partial(_win_attn_t_kernel, heads=heads, ck=ck, tq=tq, hb=hb, n_tiles=seq // tq,
                          has_sink=sink is not None, has_lse=has_lse, v_rows=v_rows),
        grid=(rows // ts,),
        in_specs=in_specs,
        out_specs=out_specs,
        out_shape=out_shape,
        compiler_params=_cparams(1),
        name=name,
    )(*args)


def _transpose_bias(bias):
    return jnp.swapaxes(bias, 1, 2) * LOG2E


def _tile_indices(nt):
    return (0, min(1, nt - 1), nt - 1)


def _banded_bias(seq, tq, hb, hw, slopes, dist_scale):
    nt = seq // tq
    tabs = []
    for ti in _tile_indices(nt):
        qpos = ti * tq + np.arange(tq)[:, None]
        kpos = ti * tq - hb + np.arange(tq + 2 * hb)[None, :]
        dist = np.abs(kpos - qpos)
        valid = (dist <= hw) & (kpos >= 0) & (kpos < seq)
        pen = -np.asarray(slopes, np.float32)[:, None, None] * (dist.astype(np.float32) * np.float32(dist_scale))[None]
        tabs.append(np.where(valid[None], pen, np.float32(NEG_INF)).reshape(-1, tq + 2 * hb))
    return jnp.asarray(np.stack(tabs).astype(np.float32))


def _na_bias(rpb, seq, rows_per_tile):
    rows = seq // GRID_W
    kh = min(NA_KH, rows)
    nt = rows // rows_per_tile
    tq = rows_per_tile * GRID_W
    cq = np.arange(GRID_W)[:, None]
    ck = np.arange(GRID_W)[None, :]
    col_start = np.clip(cq - NA_KW // 2, 0, GRID_W - NA_KW)
    col_ok = (ck >= col_start) & (ck < col_start + NA_KW)
    dc = np.clip(ck - cq, -(NA_KW - 1), NA_KW - 1) + (NA_KW - 1)
    col_hot = jnp.asarray(np.eye(2 * NA_KW - 1, dtype=np.float32)[dc])
    tabs = []
    for ti in _tile_indices(nt):
        rq = (ti * rows_per_tile + np.arange(rows_per_tile))[:, None]
        rk = ((ti - 1) * rows_per_tile + np.arange(3 * rows_per_tile))[None, :]
        row_start = np.clip(rq - kh // 2, 0, rows - kh)
        row_ok = (rk >= row_start) & (rk < row_start + kh)
        dr = np.clip(rk - rq + (NA_KH - 1), 0, 2 * NA_KH - 2)
        row_hot = jnp.asarray(np.eye(2 * NA_KH - 1, dtype=np.float32)[dr])
        vals = jnp.einsum("abr,hrc,xyc->haxby", row_hot, rpb.astype(F32), col_hot,
                          precision=lax.Precision.HIGHEST)
        valid = row_ok[:, None, :, None] & col_ok[None, :, None, :]
        tabs.append(jnp.where(valid[None], vals, NEG_INF).reshape(-1, 3 * tq))
    return jnp.stack(tabs)


def _alibi_slopes():
    s = np.exp2(-8.0 * np.arange(1, 9, dtype=np.float32) / 8.0)
    return s[1::2], s[0::2]


GQA_HEAD_ORDER = (0, 2, 1, 3)
HEADS_T_FULL = tuple((0, 64 * h, 64 * h, h) for h in range(4))
HEADS_T_GQA = ((0, 0, 0, 0), (0, 64, 64, 2), (1, 0, 0, 1), (1, 64, 64, 3))

A_HALO = 64


def _mixer_a(qkv_planes, seq):
    slopes_a, _ = _alibi_slopes()
    outs = []
    for (window, dil), qkv in zip(A_CONFIGS, qkv_planes):
        length = seq // dil
        hw = window // (2 * dil)
        tq = min(128, length)
        bias = _banded_bias(length, tq, A_HALO, hw, slopes_a, float(dil))
        outs.append(_win_attn_t(
            qkv.reshape(-1, W_A), None, _transpose_bias(bias), None, seq=length, ts=4096, tq=tq, hb=A_HALO,
            cq=GROUP_W, ck=GROUP_W, qcol=0, kcol=1, vcol=2, heads=HEADS_T_FULL, has_lse=True,
            name=f"mix_a_d{dil}"))
    return outs


def _mixer_b(qk_b, vt_b, sink, seq):
    _, slopes_b = _alibi_slopes()
    tq = 256
    bias = _banded_bias(seq, tq, B_HALF_WINDOW, B_HALF_WINDOW, slopes_b[list(GQA_HEAD_ORDER)], 1.0)
    (o,) = _win_attn_t(
        qk_b, vt_b, _transpose_bias(bias), sink, seq=seq, ts=4096, tq=tq, hb=B_HALF_WINDOW,
        cq=GROUP_W, ck=128, qcol=0, kcol=2, vcol=None, heads=HEADS_T_GQA, has_lse=False, name="mix_b")
    return o


def _mixer_d(qk_d, vt_d, rpb, seq):
    rows_per_tile = 4
    tq = rows_per_tile * GRID_W
    bias = _na_bias(rpb, seq, rows_per_tile)
    (o,) = _win_attn_t(
        qk_d, vt_d, _transpose_bias(bias), None, seq=seq, ts=4096, tq=tq, hb=tq,
        cq=GROUP_W, ck=GROUP_W, qcol=0, kcol=1, vcol=None, heads=HEADS_T_FULL, has_lse=False, name="mix_d")
    return o


def _mla_flash_kernel(q_ref, k_ref, vt_ref, o_ref, m_sc, acc_sc, s_sc, *, kc, depth):
    hp, hv = MLA_HEAD_PAD, MLA_V_ROWS
    kv = pl.program_id(2)
    tk = k_ref.shape[1]

    @pl.when(kv == 0)
    def _():
        m_sc[...] = jnp.full(m_sc.shape, NEG_INF, F32)
        acc_sc[...] = jnp.zeros(acc_sc.shape, F32)

    units = [(c, h) for c in range(tk // kc) for h in range(C_HEADS)]

    def scores(u):
        c, h = units[u]
        s_sc[u % depth] = jnp.dot(k_ref[0, c * kc:(c + 1) * kc, h * hp:(h + 1) * hp],
                                  q_ref[0, h * hp:(h + 1) * hp, :], preferred_element_type=F32)

    for u in range(min(depth, len(units))):
        scores(u)
    for u, (c, h) in enumerate(units):
        st = s_sc[u % depth]
        m_old = m_sc[h:h + 1, :]
        m_new = jnp.maximum(m_old, jnp.max(st, axis=0, keepdims=True))
        alpha = jnp.exp2(m_old - m_new)
        pt = jnp.exp2(st - m_new).astype(BF16)
        m_sc[h:h + 1, :] = m_new
        if u + depth < len(units):
            scores(u + depth)
        rows = slice(h * hv, (h + 1) * hv)
        acc_sc[rows, :] = alpha * acc_sc[rows, :] + jnp.dot(
            vt_ref[0, rows, c * kc:(c + 1) * kc], pt, preferred_element_type=F32)

    @pl.when(kv == pl.num_programs(2) - 1)
    def _():
        ot = jnp.concatenate(
            [acc_sc[h * hv:h * hv + HEAD_DIM, :] / acc_sc[h * hv + HEAD_DIM:h * hv + HEAD_DIM + 1, :]
             for h in range(C_HEADS)], axis=0)
        o_ref[0] = ot.T.astype(o_ref.dtype)


def _mla_flash(q, k, vt, tq=512, tk=4096, kc=256, depth=8):
    b, hw, s = q.shape
    tq = min(tq, s)
    tk = min(tk, s)
    vrows = C_HEADS * MLA_V_ROWS
    return pl.pallas_call(
        functools.partial(_mla_flash_kernel, kc=kc, depth=depth),
        grid=(b, s // tq, s // tk),
        in_specs=[pl.BlockSpec((1, hw, tq), lambda n, i, j: (n, 0, i)),
                  pl.BlockSpec((1, tk, hw), lambda n, i, j: (n, j, 0)),
                  pl.BlockSpec((1, vrows, tk), lambda n, i, j: (n, 0, j))],
        out_specs=pl.BlockSpec((1, tq, GROUP_W), lambda n, i, j: (n, i, 0)),
        out_shape=jax.ShapeDtypeStruct((b, s, GROUP_W), BF16),
        scratch_shapes=[pltpu.VMEM((8, tq), F32), pltpu.VMEM((vrows, tq), F32),
                        pltpu.VMEM((depth, kc, tq), F32)],
        compiler_params=_cparams(3),
        name="mla_flash",
    )(q, k, vt)


def _rope_tables(seq):
    inv_freq = 1.0 / (ROPE_THETA ** (jnp.arange(0, C_ROPE, 2, dtype=F32) / C_ROPE))
    ang = jnp.arange(seq, dtype=F32)[:, None] * inv_freq[None, :]
    cos, sin = jnp.cos(ang), jnp.sin(ang)
    pad = MLA_HEAD_PAD - C_NOPE - C_ROPE
    ct = jnp.concatenate([jnp.ones((seq, C_NOPE), F32), cos, cos, jnp.zeros((seq, pad), F32)], axis=1)
    st = jnp.concatenate([jnp.zeros((seq, C_NOPE), F32), -sin, sin, jnp.zeros((seq, pad), F32)], axis=1)
    return ct, st, ct.T, st.T


def _outproj_kernel(a1_ref, a2_ref, a3_ref, l1_ref, l2_ref, l3_ref, ob_ref, oc_ref, od_ref,
                    x_ref, g_ref, w_ref, gm_ref, wu_ref, wd_ref, gf_ref, o_ref, *scratch, final, ff_chunk):
    tm = x_ref.shape[0]
    ncol = GROUP_W // LANES

    def natural(ref, dil, sc):
        for r in range(dil):
            for c in range(ncol):
                sc[c, pl.ds(r, tm // dil, stride=dil), :] = ref[0, r, :, c * LANES:(c + 1) * LANES].astype(F32)
        return jnp.concatenate([sc[c] for c in range(ncol)], axis=1)

    a_s = [a1_ref[...].astype(F32), natural(a2_ref, 4, scratch[0]), natural(a3_ref, 16, scratch[1])]
    ls = [l1_ref[...], natural(l2_ref, 4, scratch[2]), natural(l3_ref, 16, scratch[3])]
    m = jnp.maximum(jnp.maximum(ls[0], ls[1]), ls[2])
    es = [jnp.exp(l - m) for l in ls]
    den = es[0] + es[1] + es[2]
    oa = sum((e / den) * a for e, a in zip(es, a_s))
    groups = [oa, ob_ref[...].astype(F32), oc_ref[...].astype(F32), od_ref[...].astype(F32)]
    y = jnp.concatenate([_rms(o, g_ref[i:i + 1, :]).astype(BF16) for i, o in enumerate(groups)], axis=1)
    x = x_ref[...] + jnp.dot(y, w_ref[...], preferred_element_type=F32)
    h = _rms(x, gm_ref[...]).astype(BF16)
    acc = x
    for c in range(D_FF // ff_chunk):
        cols = slice(c * ff_chunk, (c + 1) * ff_chunk)
        u = jnp.dot(h, wu_ref[:, cols], preferred_element_type=F32)
        u = jnp.square(jnp.maximum(u, 0.0)).astype(BF16)
        acc = acc + jnp.dot(u, wd_ref[cols, :], preferred_element_type=F32)
    if final:
        acc = _rms(acc, gf_ref[...])
    o_ref[...] = acc


def _outproj_mlp(a_outs, ob, oc, od, x2, lw, gf, final, bsz, seq, tm=512, ff_chunk=1024):
    t = x2.shape[0]
    nt = seq // tm
    tok = lambda n: pl.BlockSpec((tm, n), lambda b, i: (b * nt + i, 0))
    plane = lambda dil: pl.BlockSpec((1, dil, tm // dil, GROUP_W), lambda b, i: (b, 0, i, 0))
    dils = [dil for _, dil in A_CONFIGS]
    a_specs = [tok(GROUP_W) if dil == 1 else plane(dil) for dil in dils]
    planes = lambda arr, dil: arr if dil == 1 else arr.reshape(bsz, dil, seq // dil, GROUP_W)
    args = ([planes(o, dil) for (o, _), dil in zip(a_outs, dils)] + [planes(l, dil) for (_, l), dil in zip(a_outs, dils)]
            + [ob, oc, od, x2, lw["gn"], lw["w_out"], lw["norm_mlp"], lw["w_up"], lw["w_down"], gf])
    const = lambda shape: pl.BlockSpec(shape, lambda b, i: (0,) * len(shape))
    return pl.pallas_call(
        functools.partial(_outproj_kernel, final=final, ff_chunk=ff_chunk),
        grid=(bsz, nt),
        in_specs=a_specs + a_specs + [tok(GROUP_W)] * 3 + [
            tok(D_MODEL), const((N_GROUPS, GROUP_W)), const((D_MODEL, D_MODEL)),
            const((1, D_MODEL)), const((D_MODEL, D_FF)), const((D_FF, D_MODEL)), const((1, D_MODEL))],
        out_specs=tok(D_MODEL),
        out_shape=jax.ShapeDtypeStruct((t, D_MODEL), F32),
        scratch_shapes=[pltpu.VMEM((GROUP_W // LANES, tm, LANES), F32) for _ in range(4)],
        compiler_params=_cparams(2),
        name="outproj_mlp_final" if final else "outproj_mlp",
    )(*args)


def _gqa_perm():
    return np.concatenate([np.arange(h * HEAD_DIM, (h + 1) * HEAD_DIM) for h in GQA_HEAD_ORDER])


def _layer_weights(lp):
    (norm_attn, w_in, mla_q_norm, w_q_up, mla_kv_norm, w_kv_up, sink_logits, na_rpb,
     group_norm, w_out, norm_mlp, w_mlp_up, w_mlp_down) = lp
    perm = _gqa_perm()
    half = C_ROPE // 2
    z = lambda n: jnp.zeros((D_MODEL, n), w_in.dtype)
    kr = w_in[:, 1664:1696]
    kr_sw = jnp.concatenate([kr[:, half:], kr[:, :half]], axis=1)
    pad = MLA_HEAD_PAD - C_NOPE - C_ROPE
    qscale = HEAD_DIM ** -0.5 * LOG2E
    w_in2 = jnp.concatenate([
        w_in[:, 0:256] * qscale, w_in[:, 256:768],
        w_in[:, 768:1024][:, perm] * qscale, w_in[:, 1024:1152],
        w_in[:, 1696:1952] * qscale, w_in[:, 1952:2208],
        w_in[:, 1280:1664],
        z(C_NOPE), kr, z(pad), z(C_NOPE), kr_sw, z(pad),
    ], axis=1).astype(BF16)

    wq = w_q_up.reshape(C_Q_RANK, C_HEADS, C_NOPE + C_ROPE)
    zq = lambda n: jnp.zeros((C_Q_RANK, C_HEADS, n), w_q_up.dtype)
    rope = wq[:, :, C_NOPE:]
    rope_sw = jnp.concatenate([rope[:, :, half:], rope[:, :, :half]], axis=2)
    wq_main = jnp.concatenate([wq, zq(pad)], axis=2).reshape(C_Q_RANK, -1)
    wq_swap = jnp.concatenate([zq(C_NOPE), rope_sw, zq(pad)], axis=2).reshape(C_Q_RANK, -1)
    wq2 = jnp.concatenate([wq_main, wq_swap], axis=1).T.astype(BF16)

    wkv = w_kv_up.reshape(C_KV_RANK, C_HEADS, C_NOPE + HEAD_DIM)
    wk2 = jnp.concatenate([wkv[:, :, :C_NOPE], jnp.zeros((C_KV_RANK, C_HEADS, MLA_HEAD_PAD - C_NOPE), w_kv_up.dtype)],
                          axis=2).reshape(C_KV_RANK, -1).astype(BF16)
    wvt = wkv[:, :, C_NOPE:].reshape(C_KV_RANK, -1).T.astype(BF16)

    wvt_bd = jnp.concatenate([w_in[:, 1152:1280], w_in[:, 2208:2464]], axis=1).T.astype(BF16)
    return dict(
        norm_attn=norm_attn.reshape(1, -1), w_in=w_in2, wvt_bd=wvt_bd,
        qn=mla_q_norm.reshape(1, -1), kvn=mla_kv_norm.reshape(1, -1), wq=wq2, wk=wk2, wvt=wvt,
        sink=sink_logits[jnp.asarray(GQA_HEAD_ORDER)], rpb=na_rpb,
        gn=group_norm, w_out=w_out.astype(BF16), norm_mlp=norm_mlp.reshape(1, -1),
        w_up=w_mlp_up.astype(BF16), w_down=w_mlp_down.astype(BF16))


def _layer(x2, bsz, seq, lw, tables, gf, final):
    qkv_a, qkv_a4, qkv_a16, qk_b, qk_d, vt_b, vt_d, q, k, vt = _inproj(x2, lw, tables, bsz, seq)
    a_outs = _mixer_a((qkv_a, qkv_a4, qkv_a16), seq)
    ob = _mixer_b(qk_b, vt_b, lw["sink"], seq)
    oc = _mla_flash(q, k, vt).reshape(bsz * seq, GROUP_W)
    od = _mixer_d(qk_d, vt_d, lw["rpb"], seq)
    return _outproj_mlp(a_outs, ob, oc, od, x2, lw, gf, final, bsz, seq)


def _trunk(x, layer_ws, gf):
    bsz, seq, _ = x.shape
    tables = _rope_tables(seq)
    x2 = x.reshape(bsz * seq, D_MODEL)
    for i, lw in enumerate(layer_ws):
        x2 = _layer(x2, bsz, seq, lw, tables, gf, final=(i == len(layer_ws) - 1))
    return x2.reshape(bsz, seq, D_MODEL)


def kernel(x_prompt, x_sample, norm_attn, w_in, mla_q_norm, w_q_up, mla_kv_norm, w_kv_up, sink_logits, na_rpb, group_norm, w_out, norm_mlp, w_mlp_up, w_mlp_down, norm_final):
    params = (norm_attn, w_in, mla_q_norm, w_q_up, mla_kv_norm, w_kv_up, sink_logits, na_rpb,
              group_norm, w_out, norm_mlp, w_mlp_up, w_mlp_down)
    depth = norm_attn.shape[0]
    layer_ws = [_layer_weights(tuple(p[i] for p in params)) for i in range(depth)]
    gf = norm_final.reshape(1, -1)
    return (_trunk(x_prompt, layer_ws, gf), _trunk(x_sample, layer_ws, gf))
```
